```python
import jax, jax.numpy as jnp
from jax import lax
import numpy as np

D_MODEL = 1024
BATCH = 16
SEQ = 2048
DEPTH = 2
DEC_BATCH = 32
DEC_SEQ = 16
PAST_LEN = 2048

CHUNK = 64
N_MIXERS = 2
N_CONV_LAYERS = (DEPTH + 1) // 2
N_DN_LAYERS = DEPTH // 2
D_FF = 2816
CONV_W = 31
DN_HEADS = 8
DN_HEAD_DIM = 128
DN_WIDTH = DN_HEADS * DN_HEAD_DIM
SHORT_CONV_W = 4
DN_IN_WIDTH = 4 * DN_WIDTH + 2 * DN_HEADS
N_MEM = 256
XA_HEADS = 4
XA_HEAD_DIM = D_MODEL // XA_HEADS
EPS = 1e-6

kernel_name = "streaming_conformer_deltanet_step"

F32 = jnp.float32


def rms_norm(x, g):
    xf = x.astype(F32)
    y = xf * lax.rsqrt(jnp.mean(xf * xf, axis=-1, keepdims=True) + EPS)
    return (y * g.astype(F32)).astype(x.dtype)


def layer_norm(x, g, b):
    xf = x.astype(F32)
    xc = xf - jnp.mean(xf, axis=-1, keepdims=True)
    y = xc * lax.rsqrt(jnp.mean(xc * xc, axis=-1, keepdims=True) + EPS)
    return (y * g.astype(F32) + b.astype(F32)).astype(x.dtype)


def l2_normalize(x):
    return x * lax.rsqrt(jnp.sum(x * x, axis=-1, keepdims=True) + EPS)


def swiglu(h, w_gate, w_up, w_down):
    return (jax.nn.silu(h @ w_gate) * (h @ w_up)) @ w_down


def causal_depthwise_conv(x_hist, w):
    return lax.conv_general_dilated(
        x_hist, w[:, None, :].astype(x_hist.dtype), window_strides=(1,), padding='VALID',
        dimension_numbers=('NWC', 'WIO', 'NWC'), feature_group_count=x_hist.shape[-1])


def conformer_conv_module(h, conv_buf, w_in, b_in, dw, dw_b, ln_g, ln_b, w_out, b_out):
    u = h @ w_in + b_in
    val, gate = jnp.split(u, 2, axis=-1)
    u = val * jax.nn.sigmoid(gate)
    hist = jnp.concatenate([conv_buf.astype(u.dtype), u], axis=1)
    c = causal_depthwise_conv(hist, dw) + dw_b
    c = jax.nn.silu(layer_norm(c, ln_g, ln_b))
    return c @ w_out + b_out, hist[:, -(CONV_W - 1):]


def chunk_gated_delta_rule(q, k, v, g, beta, S0):
    B, L, H, _ = q.shape
    DV = v.shape[-1]
    C = min(CHUNK, L)
    n = -(-L // C)
    pad = n * C - L
    if pad:
        padf = lambda t: jnp.pad(t, [(0, 0), (0, pad)] + [(0, 0)] * (t.ndim - 2))
        q, k, v, g, beta = padf(q), padf(k), padf(v), padf(g), padf(beta)

    def blocks(t):
        t = t.reshape((B, n, C) + t.shape[2:])
        return jnp.moveaxis(t, [1, 3], [0, 2])

    q, k, v, g, beta = blocks(q), blocks(k), blocks(v), blocks(g), blocks(beta)
    gc = jnp.cumsum(g, axis=-1)
    idx = jnp.arange(C)
    causal = idx[:, None] >= idx[None, :]
    strict = idx[:, None] > idx[None, :]
    decay = jnp.exp(jnp.where(causal, gc[..., :, None] - gc[..., None, :], -jnp.inf))
    kb = k * beta[..., None]
    kk = jnp.einsum('nbhik,nbhjk->nbhij', kb, k) * decay
    a_mat = jnp.where(strict, kk, 0.0) + jnp.eye(C, dtype=F32)
    rhs = jnp.concatenate([v * beta[..., None], kb * jnp.exp(gc)[..., None]], axis=-1)
    sol = lax.linalg.triangular_solve(a_mat, rhs, left_side=True, lower=True, unit_diagonal=True)
    v_corr, k_cum = sol[..., :DV], sol[..., DV:]
    qk = jnp.einsum('nbhik,nbhjk->nbhij', q, k) * decay
    q_dec = q * jnp.exp(gc)[..., None]
    k_dec = k * jnp.exp(gc[..., -1:] - gc)[..., None]
    g_end = jnp.exp(gc[..., -1])[..., None, None]

    def step(S, xs):
        qk_i, q_i, kc_i, vc_i, kd_i, ge_i = xs
        u = vc_i - jnp.einsum('bhck,bhkv->bhcv', kc_i, S)
        o = jnp.einsum('bhck,bhkv->bhcv', q_i, S) + jnp.einsum('bhij,bhjv->bhiv', qk_i, u)
        S = S * ge_i + jnp.einsum('bhck,bhcv->bhkv', kd_i, u)
        return S, o

    S, o = lax.scan(step, S0, (qk, q_dec, k_cum, v_corr, k_dec, g_end))
    o = jnp.moveaxis(o, [0, 2], [1, 3]).reshape(B, n * C, H, DV)[:, :L]
    return o, S


def gated_deltanet(h, conv_buf, S0, w_in, conv_w, a_log, dt_bias, norm_g, w_out):
    B, L, _ = h.shape
    proj = h @ w_in
    qkv_raw = proj[..., :3 * DN_WIDTH]
    gate = proj[..., 3 * DN_WIDTH:4 * DN_WIDTH]
    a_in = proj[..., 4 * DN_WIDTH:4 * DN_WIDTH + DN_HEADS]
    b_in = proj[..., 4 * DN_WIDTH + DN_HEADS:]
    hist = jnp.concatenate([conv_buf.astype(qkv_raw.dtype), qkv_raw], axis=1)
    qkv = jax.nn.silu(causal_depthwise_conv(hist, conv_w)).astype(F32)
    q, k, v = jnp.split(qkv, 3, axis=-1)
    q = l2_normalize(q.reshape(B, L, DN_HEADS, DN_HEAD_DIM)) * (DN_HEAD_DIM ** -0.5)
    k = l2_normalize(k.reshape(B, L, DN_HEADS, DN_HEAD_DIM))
    v = v.reshape(B, L, DN_HEADS, DN_HEAD_DIM)
    beta = jax.nn.sigmoid(b_in.astype(F32))
    g = -jnp.exp(a_log.astype(F32)) * jax.nn.softplus(a_in.astype(F32) + dt_bias.astype(F32))
    o, S = chunk_gated_delta_rule(q, k, v, g, beta, S0.astype(F32))
    o = rms_norm(o, norm_g) * jax.nn.silu(gate.astype(F32).reshape(B, L, DN_HEADS, DN_HEAD_DIM))
    y = o.reshape(B, L, DN_WIDTH).astype(h.dtype) @ w_out
    return y, S, hist[:, -(SHORT_CONV_W - 1):]


def memory_cross_attention(h, mem_k, mem_v, wq, wo):
    B, L, _ = h.shape
    q = (h @ wq).reshape(B, L, XA_HEADS, XA_HEAD_DIM)
    s = jnp.einsum('blhd,bmhd->bhlm', q, mem_k).astype(F32) * (XA_HEAD_DIM ** -0.5)
    p = jax.nn.softmax(s, axis=-1).astype(mem_v.dtype)
    o = jnp.einsum('bhlm,bmhd->blhd', p, mem_v).reshape(B, L, D_MODEL)
    return o @ wo


def trunk(x, mem_k, mem_v, conv_bufs, dn_states, dn_bufs, P):
    new_conv, new_states, new_dn_bufs = [], [], []
    for i in range(DEPTH):
        h = rms_norm(x, P['norm_pre'][i, 0])
        f = swiglu(h, P['ffn_w_gate'][i, 0], P['ffn_w_up'][i, 0], P['ffn_w_down'][i, 0])
        x = x + 0.5 * rms_norm(f, P['norm_post'][i, 0])
        h = rms_norm(x, P['norm_pre'][i, 1])
        j = i // N_MIXERS
        if i % N_MIXERS == 0:
            y, buf = conformer_conv_module(
                h, conv_bufs[j], P['ca_w_in'][j], P['ca_b_in'][j], P['ca_dw'][j], P['ca_dw_b'][j],
                P['ca_ln_g'][j], P['ca_ln_b'][j], P['ca_w_out'][j], P['ca_b_out'][j])
            new_conv.append(buf)
        else:
            y, S, buf = gated_deltanet(
                h, dn_bufs[j], dn_states[j], P['dn_w_in'][j], P['dn_conv_w'][j], P['dn_a_log'][j],
                P['dn_dt_bias'][j], P['dn_norm_g'][j], P['dn_w_out'][j])
            new_states.append(S)
            new_dn_bufs.append(buf)
        x = x + rms_norm(y, P['norm_post'][i, 1])
        h = rms_norm(x, P['norm_pre'][i, 2])
        a = memory_cross_attention(h, mem_k[i], mem_v[i], P['xa_wq'][i], P['xa_wo'][i])
        x = x + rms_norm(a, P['norm_post'][i, 2])
        h = rms_norm(x, P['norm_pre'][i, 3])
        f = swiglu(h, P['ffn_w_gate'][i, 1], P['ffn_w_up'][i, 1], P['ffn_w_down'][i, 1])
        x = x + 0.5 * rms_norm(f, P['norm_post'][i, 3])
    return x, jnp.stack(new_conv), jnp.stack(new_states), jnp.stack(new_dn_bufs)


def setup_inputs(seed: int = 0) -> dict:
    key = jax.random.key(seed)
    ks = jax.random.split(key, 32)
    nrm = lambda i, shape, scale: jax.random.normal(ks[i], shape, F32) * scale
    dt = jnp.exp(jax.random.uniform(ks[28], (N_DN_LAYERS, DN_HEADS), F32, np.log(1e-3), np.log(1e-1)))
    return {
        'x_prompt': nrm(0, (BATCH, SEQ, D_MODEL), 1.0),
        'x_sample': nrm(1, (DEC_BATCH, DEC_SEQ, D_MODEL), 1.0),
        'mem_prompt': nrm(2, (BATCH, N_MEM, D_MODEL), 1.0),
        'cache_conv_a': nrm(3, (N_CONV_LAYERS, DEC_BATCH, CONV_W - 1, D_MODEL), 0.5),
        'state_dn': nrm(4, (N_DN_LAYERS, DEC_BATCH, DN_HEADS, DN_HEAD_DIM, DN_HEAD_DIM), 0.1),
        'cache_dn_conv': nrm(5, (N_DN_LAYERS, DEC_BATCH, SHORT_CONV_W - 1, 3 * DN_WIDTH), 1.0),
        'cache_mem_k': nrm(6, (DEPTH, DEC_BATCH, N_MEM, XA_HEADS, XA_HEAD_DIM), 1.0),
        'cache_mem_v': nrm(7, (DEPTH, DEC_BATCH, N_MEM, XA_HEADS, XA_HEAD_DIM), 1.0),
        'norm_pre': 1.0 + nrm(8, (DEPTH, 4, D_MODEL), 0.02),
        'norm_post': 1.0 + nrm(9, (DEPTH, 4, D_MODEL), 0.02),
        'ffn_w_gate': nrm(10, (DEPTH, 2, D_MODEL, D_FF), D_MODEL ** -0.5),
        'ffn_w_up': nrm(11, (DEPTH, 2, D_MODEL, D_FF), D_MODEL ** -0.5),
        'ffn_w_down': nrm(12, (DEPTH, 2, D_FF, D_MODEL), D_FF ** -0.5),
        'xa_wq': nrm(13, (DEPTH, D_MODEL, D_MODEL), D_MODEL ** -0.5),
        'xa_wk': nrm(14, (DEPTH, D_MODEL, D_MODEL), D_MODEL ** -0.5),
        'xa_wv': nrm(15, (DEPTH, D_MODEL, D_MODEL), D_MODEL ** -0.5),
        'xa_wo': nrm(16, (DEPTH, D_MODEL, D_MODEL), D_MODEL ** -0.5),
        'ca_w_in': nrm(17, (N_CONV_LAYERS, D_MODEL, 2 * D_MODEL), D_MODEL ** -0.5),
        'ca_b_in': nrm(18, (N_CONV_LAYERS, 2 * D_MODEL), 0.01),
        'ca_dw': nrm(19, (N_CONV_LAYERS, CONV_W, D_MODEL), CONV_W ** -0.5),
        'ca_dw_b': nrm(20, (N_CONV_LAYERS, D_MODEL), 0.01),
        'ca_ln_g': 1.0 + nrm(21, (N_CONV_LAYERS, D_MODEL), 0.02),
        'ca_ln_b': nrm(22, (N_CONV_LAYERS, D_MODEL), 0.01),
        'ca_w_out': nrm(23, (N_CONV_LAYERS, D_MODEL, D_MODEL), D_MODEL ** -0.5),
        'ca_b_out': nrm(24, (N_CONV_LAYERS, D_MODEL), 0.01),
        'dn_w_in': nrm(25, (N_DN_LAYERS, D_MODEL, DN_IN_WIDTH), D_MODEL ** -0.5),
        'dn_conv_w': nrm(26, (N_DN_LAYERS, SHORT_CONV_W, 3 * DN_WIDTH), SHORT_CONV_W ** -0.5),
        'dn_a_log': jnp.log(jax.random.uniform(ks[27], (N_DN_LAYERS, DN_HEADS), F32, 1.0, 16.0)),
        'dn_dt_bias': dt + jnp.log(-jnp.expm1(-dt)),
        'dn_norm_g': 1.0 + nrm(29, (N_DN_LAYERS, DN_HEAD_DIM), 0.02),
        'dn_w_out': nrm(30, (N_DN_LAYERS, DN_WIDTH, D_MODEL), DN_WIDTH ** -0.5),
    }


def reference(x_prompt, x_sample, mem_prompt, cache_conv_a, state_dn, cache_dn_conv, cache_mem_k, cache_mem_v,
              norm_pre, norm_post, ffn_w_gate, ffn_w_up, ffn_w_down, xa_wq, xa_wk, xa_wv, xa_wo,
              ca_w_in, ca_b_in, ca_dw, ca_dw_b, ca_ln_g, ca_ln_b, ca_w_out, ca_b_out,
              dn_w_in, dn_conv_w, dn_a_log, dn_dt_bias, dn_norm_g, dn_w_out):
    P = dict(norm_pre=norm_pre, norm_post=norm_post, ffn_w_gate=ffn_w_gate, ffn_w_up=ffn_w_up,
             ffn_w_down=ffn_w_down, xa_wq=xa_wq, xa_wo=xa_wo,
             ca_w_in=ca_w_in, ca_b_in=ca_b_in, ca_dw=ca_dw, ca_dw_b=ca_dw_b, ca_ln_g=ca_ln_g,
             ca_ln_b=ca_ln_b, ca_w_out=ca_w_out, ca_b_out=ca_b_out,
             dn_w_in=dn_w_in, dn_conv_w=dn_conv_w, dn_a_log=dn_a_log, dn_dt_bias=dn_dt_bias,
             dn_norm_g=dn_norm_g, dn_w_out=dn_w_out)
    Bp = x_prompt.shape[0]
    p_mem_k = jnp.einsum('bmd,lde->lbme', mem_prompt, xa_wk).reshape(DEPTH, Bp, N_MEM, XA_HEADS, XA_HEAD_DIM)
    p_mem_v = jnp.einsum('bmd,lde->lbme', mem_prompt, xa_wv).reshape(DEPTH, Bp, N_MEM, XA_HEADS, XA_HEAD_DIM)
    zero_conv = jnp.zeros((N_CONV_LAYERS, Bp, CONV_W - 1, D_MODEL), x_prompt.dtype)
    zero_state = jnp.zeros((N_DN_LAYERS, Bp, DN_HEADS, DN_HEAD_DIM, DN_HEAD_DIM), F32)
    zero_dn_conv = jnp.zeros((N_DN_LAYERS, Bp, SHORT_CONV_W - 1, 3 * DN_WIDTH), x_prompt.dtype)
    y_prompt, p_conv_a, p_state_dn, p_dn_conv = trunk(
        x_prompt, p_mem_k, p_mem_v, zero_conv, zero_state, zero_dn_conv, P)
    y_sample, s_conv_a, s_state_dn, s_dn_conv = trunk(
        x_sample, cache_mem_k, cache_mem_v, cache_conv_a, state_dn, cache_dn_conv, P)
    return (y_prompt, y_sample, p_conv_a, p_state_dn, p_dn_conv, p_mem_k, p_mem_v, s_conv_a, s_state_dn, s_dn_conv)
```

```python
import functools

import jax
import jax.numpy as jnp
from jax import lax
from jax.experimental import pallas as pl
from jax.experimental.pallas import tpu as pltpu

F32 = jnp.float32
BF16 = jnp.bfloat16
EPS = 1e-6

DN_HEADS = 8
DN_HEAD_DIM = 128
DN_WIDTH = DN_HEADS * DN_HEAD_DIM
DN_CHUNK = 64
DN_INV_BLOCK = 16
DN_CONV_W = 4
DN_HIST_OFF = 8
CA_CONV_W = 31
CA_HIST_OFF = 32
CA_ROW_BLOCK = 64
XA_HEADS = 4
LANES = 128
SUBLANES = 8

VMEM_LIMIT_BYTES = 56 * 1024 * 1024


def _rms(x, g):
    return x * lax.rsqrt(jnp.mean(x * x, axis=-1, keepdims=True) + EPS) * g


def _silu(x):
    return x * jax.nn.sigmoid(x)


def _dot(a, b):
    return jnp.dot(a, b, preferred_element_type=F32)


def _dot_nt(a, b):
    return lax.dot_general(a, b, (((1,), (1,)), ((), ())), preferred_element_type=F32)


def _dot_tn(a, b):
    return lax.dot_general(a, b, (((0,), (0,)), ((), ())), preferred_element_type=F32)


def _fixed(tail, *lead):
    n_tail = len(tail)
    idx = tuple(lead) + (0,) * n_tail
    return pl.BlockSpec((None,) * len(lead) + tuple(tail), lambda *_: idx, pipeline_mode=pl.Buffered(1))


def _params(*sem):
    return pltpu.CompilerParams(dimension_semantics=sem, vmem_limit_bytes=VMEM_LIMIT_BYTES)


def _ffn_kernel(x_ref, gpre_ref, gpost_ref, wg_ref, wu_ref, wd_ref, o_ref):
    x = x_ref[...]
    h = _rms(x, gpre_ref[...]).astype(BF16)
    g = _dot(h, wg_ref[...])
    u = _dot(h, wu_ref[...])
    a = (_silu(g) * u).astype(BF16)
    f = _dot(a, wd_ref[...])
    o_ref[...] = x + 0.5 * _rms(f, gpost_ref[...])


def _ffn(x2d, norm_pre, norm_post, wg, wu, wd, i, k, j, tm):
    n, d = x2d.shape
    dff = wg.shape[-1]
    return pl.pallas_call(
        _ffn_kernel,
        grid=(n // tm,),
        in_specs=[
            pl.BlockSpec((tm, d), lambda m: (m, 0)),
            _fixed((1, d), i, k),
            _fixed((1, d), i, k),
            _fixed((d, dff), i, j),
            _fixed((d, dff), i, j),
            _fixed((dff, d), i, j),
        ],
        out_specs=pl.BlockSpec((tm, d), lambda m: (m, 0)),
        out_shape=jax.ShapeDtypeStruct((n, d), F32),
        compiler_params=_params("parallel"),
        name=f"ffn_l{i}_{j}",
    )(x2d, norm_pre, norm_post, wg, wu, wd)


def _memkv_kernel(m_ref, wk_ref, wv_ref, k_ref, v_ref):
    m = m_ref[...].astype(BF16)
    k_ref[...] = _dot(m, wk_ref[...])
    v_ref[...] = _dot(m, wv_ref[...])


def _memkv(mem2d, wk, wv, tm):
    n, d = mem2d.shape
    depth = wk.shape[0]
    w_spec = pl.BlockSpec((None, d, d), lambda l, m: (l, 0, 0))
    o_spec = pl.BlockSpec((None, tm, d), lambda l, m: (l, m, 0))
    return pl.pallas_call(
        _memkv_kernel,
        grid=(depth, n // tm),
        in_specs=[pl.BlockSpec((tm, d), lambda l, m: (m, 0)), w_spec, w_spec],
        out_specs=[o_spec, o_spec],
        out_shape=[jax.ShapeDtypeStruct((depth, n, d), F32)] * 2,
        compiler_params=_params("parallel", "parallel"),
        name="mem_kv",
    )(mem2d, wk, wv)


def _xattn_kernel(x_ref, k_ref, v_ref, gpre_ref, gpost_ref, wq_ref, wo_ref, o_ref, *, bb, tl):
    d = x_ref.shape[-1]
    dh = d // XA_HEADS
    x = x_ref[...].reshape(bb * tl, d)
    h = _rms(x, gpre_ref[...]).astype(BF16)
    q = _dot(h, wq_ref[...]).astype(BF16)
    rows = []
    for b in range(bb):
        kb = k_ref[b].astype(BF16)
        vb = v_ref[b].astype(BF16)
        heads = []
        for hh in range(XA_HEADS):
            sl = slice(hh * dh, (hh + 1) * dh)
            s = _dot_nt(q[b * tl:(b + 1) * tl, sl], kb[:, sl]) * (dh ** -0.5)
            p = jnp.exp(s - jnp.max(s, axis=-1, keepdims=True))
            p = p * (1.0 / jnp.sum(p, axis=-1, keepdims=True))
            heads.append(_dot(p.astype(BF16), vb[:, sl]))
        rows.append(jnp.concatenate(heads, axis=1))
    o = jnp.concatenate(rows, axis=0) if bb > 1 else rows[0]
    a = _dot(o.astype(BF16), wo_ref[...])
    o_ref[...] = (x + _rms(a, gpost_ref[...])).reshape(bb, tl, d)


def _xattn(x, mem_k, mem_v, norm_pre, norm_post, wq, wo, i, bb, tl):
    b, l, d = x.shape
    nm = mem_k.shape[2]
    x_spec = pl.BlockSpec((bb, tl, d), lambda bi, ti: (bi, ti, 0))
    kv_spec = pl.BlockSpec((None, bb, nm, d), lambda bi, ti: (i, bi, 0, 0))
    return pl.pallas_call(
        functools.partial(_xattn_kernel, bb=bb, tl=tl),
        grid=(b // bb, l // tl),
        in_specs=[x_spec, kv_spec, kv_spec, _fixed((1, d), i, 2), _fixed((1, d), i, 2),
                  _fixed((d, d), i), _fixed((d, d), i)],
        out_specs=x_spec,
        out_shape=jax.ShapeDtypeStruct((b, l, d), F32),
        compiler_params=_params("parallel", "parallel"),
        name=f"xattn_l{i}",
    )(x, mem_k, mem_v, norm_pre, norm_post, wq, wo)


def _conv_kernel(x_ref, cbuf_ref, gpre_ref, gpost_ref, win_ref, bin_ref, dw_ref, dwb_ref, lng_ref, lnb_ref,
                 wout_ref, bout_ref, o_ref, cout_ref, hist_scr, c_scr, *, bb, tl):
    d = x_ref.shape[-1]
    m = bb * tl
    hist0 = CA_HIST_OFF - (CA_CONV_W - 1)
    t = pl.program_id(1)

    @pl.when(t == 0)
    def _():
        hist_scr[:, 0, 0:SUBLANES, :] = jnp.zeros((bb, SUBLANES, d), F32)
        hist_scr[:, 0, hist0:CA_HIST_OFF, :] = cbuf_ref[...]

    x = x_ref[...].reshape(m, d)
    h = _rms(x, gpre_ref[...]).astype(BF16)
    u = _dot(h, win_ref[...]) + bin_ref[...]
    glu = u[:, :d] * jax.nn.sigmoid(u[:, d:])
    hist_scr[:, 0, CA_HIST_OFF:CA_HIST_OFF + tl, :] = glu.reshape(bb, tl, d)
    n_shift = tl + CA_HIST_OFF - SUBLANES
    for s in range(1, SUBLANES):
        hist_scr[:, s, 0:n_shift, :] = hist_scr[:, 0, s:s + n_shift, :]

    rb = min(CA_ROW_BLOCK, tl)
    groups = rb // SUBLANES
    for b in range(bb):
        for lc in range(d // LANES):
            ls = slice(lc * LANES, (lc + 1) * LANES)
            taps = [jnp.broadcast_to(dw_ref[j:j + 1, ls], (SUBLANES, LANES)) for j in range(CA_CONV_W)]
            bias = jnp.broadcast_to(dwb_ref[:, ls], (SUBLANES, LANES))

            def row_block(r, carry, b=b, ls=ls, taps=taps, bias=bias):
                r0 = pl.multiple_of(r * rb, rb)
                for g in range(groups):
                    acc = bias
                    for j in range(CA_CONV_W):
                        s, a = (hist0 + j) % SUBLANES, (hist0 + j) // SUBLANES
                        acc = acc + taps[j] * hist_scr[b, s, pl.ds(r0 + (a + g) * SUBLANES, SUBLANES), ls]
                    c_scr[pl.ds(b * tl + r0 + g * SUBLANES, SUBLANES), ls] = acc
                return carry
            lax.fori_loop(0, tl // rb, row_block, 0)

    c = c_scr[...]
    mu = jnp.mean(c, axis=-1, keepdims=True)
    cc = c - mu
    ln = cc * lax.rsqrt(jnp.mean(cc * cc, axis=-1, keepdims=True) + EPS) * lng_ref[...] + lnb_ref[...]
    y = _dot(_silu(ln).astype(BF16), wout_ref[...]) + bout_ref[...]
    o_ref[...] = (x + _rms(y, gpost_ref[...])).reshape(bb, tl, d)

    tail = hist_scr[:, 0, tl + hist0:tl + CA_HIST_OFF, :]
    hist_scr[:, 0, hist0:CA_HIST_OFF, :] = tail

    @pl.when(t == pl.num_programs(1) - 1)
    def _():
        cout_ref[...] = tail


def _conv(x, conv_buf, norm_pre, norm_post, win, b_in, dw, dw_b, ln_g, ln_b, wout, b_out, i, j, bb, tl):
    b, l, d = x.shape
    nh = CA_CONV_W - 1
    x_spec = pl.BlockSpec((bb, tl, d), lambda bi, ti: (bi, ti, 0))
    c_spec = pl.BlockSpec((None, bb, nh, d), lambda bi, ti: (j, bi, 0, 0))
    return pl.pallas_call(
        functools.partial(_conv_kernel, bb=bb, tl=tl),
        grid=(b // bb, l // tl),
        in_specs=[x_spec, c_spec, _fixed((1, d), i, 1), _fixed((1, d), i, 1),
                  _fixed((d, 2 * d), j), _fixed((1, 2 * d), j), _fixed((CA_CONV_W, d), j), _fixed((1, d), j),
                  _fixed((1, d), j), _fixed((1, d), j), _fixed((d, d), j), _fixed((1, d), j)],
        out_specs=[x_spec, pl.BlockSpec((bb, nh, d), lambda bi, ti: (bi, 0, 0))],
        out_shape=[jax.ShapeDtypeStruct((b, l, d), F32), jax.ShapeDtypeStruct((b, nh, d), F32)],
        scratch_shapes=[pltpu.VMEM((bb, SUBLANES, CA_HIST_OFF + tl, d), F32), pltpu.VMEM((bb * tl, d), F32)],
        compiler_params=_params("parallel", "arbitrary"),
        name=f"conv_l{i}",
    )(x, conv_buf, norm_pre, norm_post, win, b_in, dw, dw_b, ln_g, ln_b, wout, b_out)


def _unit_lower_inverse(n, c):
    ri = lax.broadcasted_iota(jnp.int32, (c, c), 0)
    ci = lax.broadcasted_iota(jnp.int32, (c, c), 1)
    shift = DN_INV_BLOCK.bit_length() - 1
    eye = (ri == ci).astype(F32)
    mm = lambda a, b: _dot(a.astype(BF16), b.astype(BF16))
    nd = jnp.where((ri >> shift) == (ci >> shift), n, 0.0)
    p = eye - nd
    xp = nd
    size = 2
    while size < DN_INV_BLOCK:
        xp = mm(xp, xp)
        p = p + mm(p, xp)
        size *= 2
    size = DN_INV_BLOCK
    while size < c:
        s = size.bit_length() - 1
        rblk = ri >> s
        off = ((rblk & 1) == 1) & ((ci >> s) == rblk - 1)
        a = jnp.where(off, n, 0.0)
        p = p - mm(p, mm(a, p))
        size *= 2
    return p


def _dn_kernel(x_ref, s0_ref, cbuf_ref, gpre_ref, gpost_ref, wqkvg_ref, wab_ref, cw_ref, alog_ref, dtb_ref,
               ng_ref, wout_ref, o_ref, sout_ref, cout_ref,
               hist_scr, q_scr, k_scr, v_scr, gc_scr, beta_scr, gct_scr, oacc_scr, s_scr, *, bb, tl):
    d = x_ref.shape[-1]
    hd = DN_HEAD_DIM
    w = DN_WIDTH
    m = bb * tl
    c = DN_CHUNK
    nc = -(-tl // c)
    cp = nc * c
    grp = min(c, tl)
    hist0 = DN_HIST_OFF - (DN_CONV_W - 1)
    t = pl.program_id(1)

    @pl.when(t == 0)
    def _():
        s_scr[...] = s0_ref[...]
        hist_scr[:, hist0:DN_HIST_OFF, :] = cbuf_ref[...]

    x = x_ref[...].reshape(m, d)
    h = _rms(x, gpre_ref[...]).astype(BF16)
    proj = _dot(h, wqkvg_ref[...])
    ab = _dot(h, wab_ref[...])
    hist_scr[:, DN_HIST_OFF:DN_HIST_OFF + tl, :] = proj[:, :3 * w].reshape(bb, tl, 3 * w)
    conv = cw_ref[0:1, :] * hist_scr[:, hist0:hist0 + tl, :]
    for j in range(1, DN_CONV_W):
        conv = conv + cw_ref[j:j + 1, :] * hist_scr[:, hist0 + j:hist0 + j + tl, :]
    qkv = _silu(conv)
    tail = hist_scr[:, tl + hist0:tl + DN_HIST_OFF, :]
    hist_scr[:, hist0:DN_HIST_OFF, :] = tail

    if cp != tl:
        q_scr[...] = jnp.zeros_like(q_scr)
        k_scr[...] = jnp.zeros_like(k_scr)
        v_scr[...] = jnp.zeros_like(v_scr)
        gc_scr[...] = jnp.zeros_like(gc_scr)
        beta_scr[...] = jnp.zeros_like(beta_scr)
        gct_scr[...] = jnp.zeros_like(gct_scr)

    for hh in range(DN_HEADS):
        lo = hh * hd
        qh = qkv[:, :, lo:lo + hd]
        kh = qkv[:, :, w + lo:w + lo + hd]
        q_scr[:, 0:tl, lo:lo + hd] = qh * lax.rsqrt(jnp.sum(qh * qh, axis=-1, keepdims=True) + EPS) * (hd ** -0.5)
        k_scr[:, 0:tl, lo:lo + hd] = kh * lax.rsqrt(jnp.sum(kh * kh, axis=-1, keepdims=True) + EPS)
    v_scr[:, 0:tl, :] = qkv[:, :, 2 * w:3 * w]

    abt = ab.T
    g_t = -jnp.exp(alog_ref[...]) * jax.nn.softplus(abt + dtb_ref[...])
    pos = lax.broadcasted_iota(jnp.int32, g_t.shape, 1) & (grp - 1)
    gc_t = g_t
    step = 1
    while step < grp:
        gc_t = gc_t + jnp.where(pos >= step, pltpu.roll(gc_t, step, axis=1), 0.0)
        step *= 2
    gc_scr[:, 0:tl, :] = gc_t.T.reshape(bb, tl, LANES)
    beta_scr[:, 0:tl, :] = jax.nn.sigmoid(ab).reshape(bb, tl, LANES)
    for ci in range(m // grp):
        gct_scr[ci, :, 0:grp] = gc_t[0:DN_HEADS, ci * grp:(ci + 1) * grp]

    ri = lax.broadcasted_iota(jnp.int32, (c, c), 0)
    ci_ = lax.broadcasted_iota(jnp.int32, (c, c), 1)
    causal = ri >= ci_
    strict = ri > ci_
    last = grp - 1

    def chunk(idx, carry):
        b = idx // nc if bb > 1 else 0
        r0 = pl.multiple_of((idx % nc) * c, c) if nc > 1 else 0
        gc_c = gc_scr[b, pl.ds(r0, c), :]
        beta_c = beta_scr[b, pl.ds(r0, c), :]
        gct_c = gct_scr[idx]
        g_last = gc_c[last:last + 1, :]
        eg_c = jnp.exp(gc_c)
        egl_c = jnp.exp(g_last - gc_c)
        ge_c = jnp.exp(g_last)
        for hh in range(DN_HEADS):
            sl = slice(hh * hd, (hh + 1) * hd)
            q = q_scr[b, pl.ds(r0, c), sl]
            k = k_scr[b, pl.ds(r0, c), sl]
            v = v_scr[b, pl.ds(r0, c), sl]
            beta = beta_c[:, DN_HEADS + hh:DN_HEADS + hh + 1]
            eg = eg_c[:, hh:hh + 1]
            dec = jnp.where(causal, jnp.exp(jnp.minimum(gc_c[:, hh:hh + 1] - gct_c[hh:hh + 1, :], 0.0)), 0.0)
            kb = k * beta
            kq = jnp.concatenate([kb, q], axis=0).astype(BF16)
            kkqk = _dot_nt(kq, k.astype(BF16))
            a_strict = jnp.where(strict, kkqk[:c] * dec, 0.0)
            qk = kkqk[c:] * dec
            t_inv = _unit_lower_inverse(a_strict, c)
            rhs = jnp.concatenate([v * beta, kb * eg], axis=1).astype(BF16)
            sol = _dot(t_inv.astype(BF16), rhs)
            s_old = s_scr[b, hh]
            kcq = jnp.concatenate([sol[:, hd:], q * eg], axis=0).astype(BF16)
            r = _dot(kcq, s_old.astype(BF16))
            u = sol[:, :hd] - r[:c]
            ub = u.astype(BF16)
            oacc_scr[b, pl.ds(r0, c), sl] = r[c:] + _dot(qk.astype(BF16), ub)
            kd = (k * egl_c[:, hh:hh + 1]).astype(BF16)
            s_scr[b, hh] = s_old * ge_c[:, hh:hh + 1] + _dot_tn(kd, ub)
        return carry

    lax.fori_loop(0, bb * nc, chunk, 0)

    o = oacc_scr[:, 0:tl, :].reshape(m, w)
    gate = proj[:, 3 * w:]
    ng = ng_ref[...]
    outs = []
    for hh in range(DN_HEADS):
        sl = slice(hh * hd, (hh + 1) * hd)
        outs.append(_rms(o[:, sl], ng) * _silu(gate[:, sl]))
    y = _dot(jnp.concatenate(outs, axis=1).astype(BF16), wout_ref[...])
    o_ref[...] = (x + _rms(y, gpost_ref[...])).reshape(bb, tl, d)

    @pl.when(t == pl.num_programs(1) - 1)
    def _():
        sout_ref[...] = s_scr[...]
        cout_ref[...] = tail


def _dn(x, s0, conv_buf, norm_pre, norm_post, wqkvg, wab, cw, alog, dtb, ng, wout, i, j, bb, tl):
    b, l, d = x.shape
    hd, nh, w = DN_HEAD_DIM, DN_HEADS, DN_WIDTH
    nhist = DN_CONV_W - 1
    nc = -(-tl // DN_CHUNK)
    cp = nc * DN_CHUNK
    x_spec = pl.BlockSpec((bb, tl, d), lambda bi, ti: (bi, ti, 0))
    s_in_spec = pl.BlockSpec((None, bb, nh, hd, hd), lambda bi, ti: (j, bi, 0, 0, 0))
    c_in_spec = pl.BlockSpec((None, bb, nhist, 3 * w), lambda bi, ti: (j, bi, 0, 0))
    return pl.pallas_call(
        functools.partial(_dn_kernel, bb=bb, tl=tl),
        grid=(b // bb, l // tl),
        in_specs=[x_spec, s_in_spec, c_in_spec, _fixed((1, d), i, 1), _fixed((1, d), i, 1),
                  _fixed((d, 4 * w), j), _fixed((d, LANES), j), _fixed((DN_CONV_W, 3 * w), j),
                  _fixed((LANES, 1), j), _fixed((LANES, 1), j), _fixed((1, hd), j), _fixed((w, d), j)],
        out_specs=[x_spec,
                   pl.BlockSpec((bb, nh, hd, hd), lambda bi, ti: (bi, 0, 0, 0)),
                   pl.BlockSpec((bb, nhist, 3 * w), lambda bi, ti: (bi, 0, 0))],
        out_shape=[jax.ShapeDtypeStruct((b, l, d), F32),
                   jax.ShapeDtypeStruct((b, nh, hd, hd), F32),
                   jax.ShapeDtypeStruct((b, nhist, 3 * w), F32)],
        scratch_shapes=[pltpu.VMEM((bb, DN_HIST_OFF + tl, 3 * w), F32),
                        pltpu.VMEM((bb, cp, w), F32), pltpu.VMEM((bb, cp, w), F32), pltpu.VMEM((bb, cp, w), F32),
                        pltpu.VMEM((bb, cp, LANES), F32), pltpu.VMEM((bb, cp, LANES), F32),
                        pltpu.VMEM((bb * nc, nh, DN_CHUNK), F32),
                        pltpu.VMEM((bb, cp, w), F32),
                        pltpu.VMEM((bb, nh, hd, hd), F32)],
        compiler_params=_params("parallel", "arbitrary"),
        name=f"deltanet_l{i}",
    )(x, s0, conv_buf, norm_pre, norm_post, wqkvg, wab, cw, alog, dtb, ng, wout)


def _tiles(b, l):
    if l >= 512:
        return 512, 1, 512, 256
    bb = max(1, min(b, LANES // l))
    return min(b * l, 512), bb, l, l


def _trunk(x, mem_k, mem_v, conv_bufs, dn_states, dn_bufs, P):
    b, l, d = x.shape
    depth = P["wq"].shape[0]
    tm, bb, tl, tl_dn = _tiles(b, l)
    ffn = lambda x, i, k, j: _ffn(x.reshape(b * l, d), P["norm_pre"], P["norm_post"], P["wg"], P["wu"], P["wd"],
                                  i, k, j, tm).reshape(b, l, d)
    new_conv, new_states, new_dn_bufs = [], [], []
    for i in range(depth):
        x = ffn(x, i, 0, 0)
        j = i // 2
        if i % 2 == 0:
            x, buf = _conv(x, conv_bufs, P["norm_pre"], P["norm_post"], P["ca_win"], P["ca_b_in"], P["ca_dw"],
                           P["ca_dw_b"], P["ca_ln_g"], P["ca_ln_b"], P["ca_wout"], P["ca_b_out"], i, j, bb, tl)
            new_conv.append(buf)
        else:
            x, s, buf = _dn(x, dn_states, dn_bufs, P["norm_pre"], P["norm_post"], P["dn_wqkvg"], P["dn_wab"],
                            P["dn_conv_w"], P["dn_alog"], P["dn_dtb"], P["dn_norm_g"], P["dn_wout"], i, j, bb, tl_dn)
            new_states.append(s)
            new_dn_bufs.append(buf)
        x = _xattn(x, mem_k, mem_v, P["norm_pre"], P["norm_post"], P["wq"], P["wo"], i, bb, tl)
        x = ffn(x, i, 3, 1)
    return x, jnp.stack(new_conv), jnp.stack(new_states), jnp.stack(new_dn_bufs)


def kernel(x_prompt, x_sample, mem_prompt, cache_conv_a, state_dn, cache_dn_conv, cache_mem_k, cache_mem_v,
           norm_pre, norm_post, ffn_w_gate, ffn_w_up, ffn_w_down, xa_wq, xa_wk, xa_wv, xa_wo,
           ca_w_in, ca_b_in, ca_dw, ca_dw_b, ca_ln_g, ca_ln_b, ca_w_out, ca_b_out,
           dn_w_in, dn_conv_w, dn_a_log, dn_dt_bias, dn_norm_g, dn_w_out):
    depth, _, d = norm_pre.shape
    bp, _, _ = x_prompt.shape
    bs = x_sample.shape[0]
    nm = mem_prompt.shape[1]
    n_conv, n_dn = ca_w_in.shape[0], dn_w_in.shape[0]
    w, nh = DN_WIDTH, DN_HEADS
    row = lambda p: p[..., None, :]
    col = lambda p: jnp.pad(p, ((0, 0), (0, LANES - nh)))[..., None]
    P = dict(
        norm_pre=row(norm_pre), norm_post=row(norm_post),
        wg=ffn_w_gate.astype(BF16), wu=ffn_w_up.astype(BF16), wd=ffn_w_down.astype(BF16),
        wq=xa_wq.astype(BF16), wo=xa_wo.astype(BF16),
        ca_win=ca_w_in.astype(BF16), ca_b_in=row(ca_b_in), ca_dw=ca_dw, ca_dw_b=row(ca_dw_b),
        ca_ln_g=row(ca_ln_g), ca_ln_b=row(ca_ln_b), ca_wout=ca_w_out.astype(BF16), ca_b_out=row(ca_b_out),
        dn_wqkvg=dn_w_in[:, :, :4 * w].astype(BF16),
        dn_wab=jnp.pad(dn_w_in[:, :, 4 * w:], ((0, 0), (0, 0), (0, LANES - 2 * nh))).astype(BF16),
        dn_conv_w=dn_conv_w, dn_alog=col(dn_a_log), dn_dtb=col(dn_dt_bias), dn_norm_g=row(dn_norm_g),
        dn_wout=dn_w_out.astype(BF16),
    )
    p_k, p_v = _memkv(mem_prompt.reshape(bp * nm, d), xa_wk.astype(BF16), xa_wv.astype(BF16), 512)
    p_k = p_k.reshape(depth, bp, nm, d)
    p_v = p_v.reshape(depth, bp, nm, d)
    zero_conv = jnp.zeros((n_conv, bp) + cache_conv_a.shape[2:], F32)
    zero_state = jnp.zeros((n_dn, bp) + state_dn.shape[2:], F32)
    zero_dn_conv = jnp.zeros((n_dn, bp) + cache_dn_conv.shape[2:], F32)
    y_p, p_conv_a, p_state_dn, p_dn_conv = _trunk(x_prompt, p_k, p_v, zero_conv, zero_state, zero_dn_conv, P)
    y_s, s_conv_a, s_state_dn, s_dn_conv = _trunk(
        x_sample, cache_mem_k.reshape(depth, bs, nm, d), cache_mem_v.reshape(depth, bs, nm, d),
        cache_conv_a, state_dn, cache_dn_conv, P)
    kv_shape = (depth, bp) + cache_mem_k.shape[2:]
    return (y_p, y_s, p_conv_a, p_state_dn, p_dn_conv, p_k.reshape(kv_shape), p_v.reshape(kv_shape),
            s_conv_a, s_state_dn, s_dn_conv)
```

```python
import functools

import jax
import jax.numpy as jnp
from jax import lax
from jax.experimental import pallas as pl
from jax.experimental.pallas import tpu as pltpu

F32 = jnp.float32
BF16 = jnp.bfloat16
EPS = 1e-6

DN_HEADS = 8
DN_HEAD_DIM = 128
DN_WIDTH = DN_HEADS * DN_HEAD_DIM
DN_CHUNK = 64
DN_INV_BLOCK = 16
DN_CONV_W = 4
DN_HIST_OFF = 8
CA_CONV_W = 31
CA_HIST_OFF = 32
CA_ROW_BLOCK = 64
XA_HEADS = 4
LANES = 128
SUBLANES = 8

VMEM_LIMIT_BYTES = 56 * 1024 * 1024


def _rms(x, g):
    return x * lax.rsqrt(jnp.mean(x * x, axis=-1, keepdims=True) + EPS) * g


def _silu(x):
    return x * jax.nn.sigmoid(x)


def _dot(a, b):
    return jnp.dot(a, b, preferred_element_type=F32)


def _dot_nt(a, b):
    return lax.dot_general(a, b, (((1,), (1,)), ((), ())), preferred_element_type=F32)


def _dot_tn(a, b):
    return lax.dot_general(a, b, (((0,), (0,)), ((), ())), preferred_element_type=F32)


def _fixed(tail, *lead):
    n_tail = len(tail)
    idx = tuple(lead) + (0,) * n_tail
    return pl.BlockSpec((None,) * len(lead) + tuple(tail), lambda *_: idx, pipeline_mode=pl.Buffered(1))


def _params(*sem):
    return pltpu.CompilerParams(dimension_semantics=sem, vmem_limit_bytes=VMEM_LIMIT_BYTES)


def _ffn_kernel(x_ref, gpre_ref, gpost_ref, wg_ref, wu_ref, wd_ref, o_ref):
    x = x_ref[...]
    h = _rms(x, gpre_ref[...]).astype(BF16)
    g = _dot(h, wg_ref[...])
    u = _dot(h, wu_ref[...])
    a = (_silu(g) * u).astype(BF16)
    f = _dot(a, wd_ref[...])
    o_ref[...] = x + 0.5 * _rms(f, gpost_ref[...])


def _ffn(x2d, norm_pre, norm_post, wg, wu, wd, i, k, j, tm):
    n, d = x2d.shape
    dff = wg.shape[-1]
    return pl.pallas_call(
        _ffn_kernel,
        grid=(n // tm,),
        in_specs=[
            pl.BlockSpec((tm, d), lambda m: (m, 0)),
            _fixed((1, d), i, k),
            _fixed((1, d), i, k),
            _fixed((d, dff), i, j),
            _fixed((d, dff), i, j),
            _fixed((dff, d), i, j),
        ],
        out_specs=pl.BlockSpec((tm, d), lambda m: (m, 0)),
        out_shape=jax.ShapeDtypeStruct((n, d), F32),
        compiler_params=_params("parallel"),
        name=f"ffn_l{i}_{j}",
    )(x2d, norm_pre, norm_post, wg, wu, wd)


def _memkv_kernel(m_ref, wk_ref, wv_ref, k_ref, v_ref):
    m = m_ref[...].astype(BF16)
    k_ref[...] = _dot(m, wk_ref[...])
    v_ref[...] = _dot(m, wv_ref[...])


def _memkv(mem2d, wk, wv, tm):
    n, d = mem2d.shape
    depth = wk.shape[0]
    w_spec = pl.BlockSpec((None, d, d), lambda l, m: (l, 0, 0))
    o_spec = pl.BlockSpec((None, tm, d), lambda l, m: (l, m, 0))
    return pl.pallas_call(
        _memkv_kernel,
        grid=(depth, n // tm),
        in_specs=[pl.BlockSpec((tm, d), lambda l, m: (m, 0)), w_spec, w_spec],
        out_specs=[o_spec, o_spec],
        out_shape=[jax.ShapeDtypeStruct((depth, n, d), F32)] * 2,
        compiler_params=_params("parallel", "parallel"),
        name="mem_kv",
    )(mem2d, wk, wv)


def _xattn_kernel(x_ref, k_ref, v_ref, gpre_ref, gpost_ref, wq_ref, wo_ref, o_ref, *, bb, tl):
    d = x_ref.shape[-1]
    dh = d // XA_HEADS
    x = x_ref[...].reshape(bb * tl, d)
    h = _rms(x, gpre_ref[...]).astype(BF16)
    q = _dot(h, wq_ref[...]).astype(BF16)
    rows = []
    for b in range(bb):
        kb = k_ref[b].astype(BF16)
        vb = v_ref[b].astype(BF16)
        heads = []
        for hh in range(XA_HEADS):
            sl = slice(hh * dh, (hh + 1) * dh)
            s = _dot_nt(q[b * tl:(b + 1) * tl, sl], kb[:, sl]) * (dh ** -0.5)
            p = jnp.exp(s - jnp.max(s, axis=-1, keepdims=True))
            p = p * (1.0 / jnp.sum(p, axis=-1, keepdims=True))
            heads.append(_dot(p.astype(BF16), vb[:, sl]))
        rows.append(jnp.concatenate(heads, axis=1))
    o = jnp.concatenate(rows, axis=0) if bb > 1 else rows[0]
    a = _dot(o.astype(BF16), wo_ref[...])
    o_ref[...] = (x + _rms(a, gpost_ref[...])).reshape(bb, tl, d)


def _xattn(x, mem_k, mem_v, norm_pre, norm_post, wq, wo, i, bb, tl):
    b, l, d = x.shape
    nm = mem_k.shape[2]
    x_spec = pl.BlockSpec((bb, tl, d), lambda bi, ti: (bi, ti, 0))
    kv_spec = pl.BlockSpec((None, bb, nm, d), lambda bi, ti: (i, bi, 0, 0))
    return pl.pallas_call(
        functools.partial(_xattn_kernel, bb=bb, tl=tl),
        grid=(b // bb, l // tl),
        in_specs=[x_spec, kv_spec, kv_spec, _fixed((1, d), i, 2), _fixed((1, d), i, 2),
                  _fixed((d, d), i), _fixed((d, d), i)],
        out_specs=x_spec,
        out_shape=jax.ShapeDtypeStruct((b, l, d), F32),
        compiler_params=_params("parallel", "parallel"),
        name=f"xattn_l{i}",
    )(x, mem_k, mem_v, norm_pre, norm_post, wq, wo)


def _conv_kernel(x_ref, cbuf_ref, gpre_ref, gpost_ref, win_ref, bin_ref, dw_ref, dwb_ref, lng_ref, lnb_ref,
                 wout_ref, bout_ref, o_ref, cout_ref, hist_scr, c_scr, *, bb, tl):
    d = x_ref.shape[-1]
    m = bb * tl
    hist0 = CA_HIST_OFF - (CA_CONV_W - 1)
    t = pl.program_id(1)

    @pl.when(t == 0)
    def _():
        hist_scr[:, 0, 0:SUBLANES, :] = jnp.zeros((bb, SUBLANES, d), F32)
        hist_scr[:, 0, hist0:CA_HIST_OFF, :] = cbuf_ref[...]

    x = x_ref[...].reshape(m, d)
    h = _rms(x, gpre_ref[...]).astype(BF16)
    u = _dot(h, win_ref[...]) + bin_ref[...]
    glu = u[:, :d] * jax.nn.sigmoid(u[:, d:])
    hist_scr[:, 0, CA_HIST_OFF:CA_HIST_OFF + tl, :] = glu.reshape(bb, tl, d)
    n_shift = tl + CA_HIST_OFF - SUBLANES
    for s in range(1, SUBLANES):
        hist_scr[:, s, 0:n_shift, :] = hist_scr[:, 0, s:s + n_shift, :]

    rb = min(CA_ROW_BLOCK, tl)
    groups = rb // SUBLANES
    for b in range(bb):
        for lc in range(d // LANES):
            ls = slice(lc * LANES, (lc + 1) * LANES)
            taps = [jnp.broadcast_to(dw_ref[j:j + 1, ls], (SUBLANES, LANES)) for j in range(CA_CONV_W)]
            bias = jnp.broadcast_to(dwb_ref[:, ls], (SUBLANES, LANES))

            def row_block(r, carry, b=b, ls=ls, taps=taps, bias=bias):
                r0 = pl.multiple_of(r * rb, rb)
                for g in range(groups):
                    acc = bias
                    for j in range(CA_CONV_W):
                        s, a = (hist0 + j) % SUBLANES, (hist0 + j) // SUBLANES
                        acc = acc + taps[j] * hist_scr[b, s, pl.ds(r0 + (a + g) * SUBLANES, SUBLANES), ls]
                    c_scr[pl.ds(b * tl + r0 + g * SUBLANES, SUBLANES), ls] = acc
                return carry
            lax.fori_loop(0, tl // rb, row_block, 0)

    c = c_scr[...]
    mu = jnp.mean(c, axis=-1, keepdims=True)
    cc = c - mu
    ln = cc * lax.rsqrt(jnp.mean(cc * cc, axis=-1, keepdims=True) + EPS) * lng_ref[...] + lnb_ref[...]
    y = _dot(_silu(ln).astype(BF16), wout_ref[...]) + bout_ref[...]
    o_ref[...] = (x + _rms(y, gpost_ref[...])).reshape(bb, tl, d)

    tail = hist_scr[:, 0, tl + hist0:tl + CA_HIST_OFF, :]
    hist_scr[:, 0, hist0:CA_HIST_OFF, :] = tail

    @pl.when(t == pl.num_programs(1) - 1)
    def _():
        cout_ref[...] = tail


def _conv(x, conv_buf, norm_pre, norm_post, win, b_in, dw, dw_b, ln_g, ln_b, wout, b_out, i, j, bb, tl):
    b, l, d = x.shape
    nh = CA_CONV_W - 1
    x_spec = pl.BlockSpec((bb, tl, d), lambda bi, ti: (bi, ti, 0))
    c_spec = pl.BlockSpec((None, bb, nh, d), lambda bi, ti: (j, bi, 0, 0))
    return pl.pallas_call(
        functools.partial(_conv_kernel, bb=bb, tl=tl),
        grid=(b // bb, l // tl),
        in_specs=[x_spec, c_spec, _fixed((1, d), i, 1), _fixed((1, d), i, 1),
                  _fixed((d, 2 * d), j), _fixed((1, 2 * d), j), _fixed((CA_CONV_W, d), j), _fixed((1, d), j),
                  _fixed((1, d), j), _fixed((1, d), j), _fixed((d, d), j), _fixed((1, d), j)],
        out_specs=[x_spec, pl.BlockSpec((bb, nh, d), lambda bi, ti: (bi, 0, 0))],
        out_shape=[jax.ShapeDtypeStruct((b, l, d), F32), jax.ShapeDtypeStruct((b, nh, d), F32)],
        scratch_shapes=[pltpu.VMEM((bb, SUBLANES, CA_HIST_OFF + tl, d), F32), pltpu.VMEM((bb * tl, d), F32)],
        compiler_params=_params("parallel", "arbitrary"),
        name=f"conv_l{i}",
    )(x, conv_buf, norm_pre, norm_post, win, b_in, dw, dw_b, ln_g, ln_b, wout, b_out)


def _unit_lower_inverses(ns, c):
    ri = lax.broadcasted_iota(jnp.int32, (c, c), 0)
    ci = lax.broadcasted_iota(jnp.int32, (c, c), 1)
    shift = DN_INV_BLOCK.bit_length() - 1
    eye = (ri == ci).astype(F32)
    same = (ri >> shift) == (ci >> shift)
    mm = lambda a, b: _dot(a.astype(BF16), b.astype(BF16))
    xs = [jnp.where(same, n, 0.0) for n in ns]
    ps = [eye - x for x in xs]
    size = 2
    while size < DN_INV_BLOCK:
        xs = [mm(x, x) for x in xs]
        ps = [p + px for p, px in zip(ps, [mm(p, x) for p, x in zip(ps, xs)])]
        size *= 2
    size = DN_INV_BLOCK
    while size < c:
        s = size.bit_length() - 1
        rblk = ri >> s
        off = ((rblk & 1) == 1) & ((ci >> s) == rblk - 1)
        aps = [mm(jnp.where(off, n, 0.0), p) for n, p in zip(ns, ps)]
        ps = [p - pap for p, pap in zip(ps, [mm(p, ap) for p, ap in zip(ps, aps)])]
        size *= 2
    return ps


def _dn_kernel(x_ref, s0_ref, cbuf_ref, gpre_ref, gpost_ref, wqkvg_ref, wab_ref, cw_ref, alog_ref, dtb_ref,
               ng_ref, wout_ref, o_ref, sout_ref, cout_ref, hist_scr, s_scr, *, bb, tl):
    d = x_ref.shape[-1]
    hd = DN_HEAD_DIM
    w = DN_WIDTH
    m = bb * tl
    c = DN_CHUNK
    nc = -(-tl // c)
    grp = min(c, tl)
    nck = bb * nc
    hist0 = DN_HIST_OFF - (DN_CONV_W - 1)
    t = pl.program_id(1)

    @pl.when(t == 0)
    def _():
        s_scr[...] = s0_ref[...]
        hist_scr[:, hist0:DN_HIST_OFF, :] = cbuf_ref[...]

    x = x_ref[...].reshape(m, d)
    h = _rms(x, gpre_ref[...]).astype(BF16)
    proj = _dot(h, wqkvg_ref[...])
    ab = _dot(h, wab_ref[...])
    hist_scr[:, DN_HIST_OFF:DN_HIST_OFF + tl, :] = proj[:, :3 * w].reshape(bb, tl, 3 * w)
    conv = cw_ref[0:1, :] * hist_scr[:, hist0:hist0 + tl, :]
    for j in range(1, DN_CONV_W):
        conv = conv + cw_ref[j:j + 1, :] * hist_scr[:, hist0 + j:hist0 + j + tl, :]
    qkv = _silu(conv).reshape(m, 3 * w)
    tail = hist_scr[:, tl + hist0:tl + DN_HIST_OFF, :]
    hist_scr[:, hist0:DN_HIST_OFF, :] = tail

    abt = ab.T
    g_t = -jnp.exp(alog_ref[...]) * jax.nn.softplus(abt + dtb_ref[...])
    pos = lax.broadcasted_iota(jnp.int32, g_t.shape, 1) & (grp - 1)
    gc_t = g_t
    step = 1
    while step < grp:
        gc_t = gc_t + jnp.where(pos >= step, pltpu.roll(gc_t, step, axis=1), 0.0)
        step *= 2
    gc_col = gc_t.T
    beta_col = jax.nn.sigmoid(ab)

    ri = lax.broadcasted_iota(jnp.int32, (c, c), 0)
    ci_ = lax.broadcasted_iota(jnp.int32, (c, c), 1)
    causal = ri >= ci_
    strict = ri > ci_
    last = grp - 1
    bf = lambda a: a.astype(BF16)

    def chunk_rows(a, ck):
        blk = a[ck * grp:(ck + 1) * grp]
        if grp == c:
            return blk
        return jnp.concatenate([blk, jnp.zeros((c - grp, a.shape[1]), a.dtype)], axis=0)

    qn, kn = [], []
    for hh in range(DN_HEADS):
        qh = qkv[:, hh * hd:(hh + 1) * hd]
        kh = qkv[:, w + hh * hd:w + (hh + 1) * hd]
        qn.append(qh * lax.rsqrt(jnp.sum(qh * qh, axis=-1, keepdims=True) + EPS) * (hd ** -0.5))
        kn.append(kh * lax.rsqrt(jnp.sum(kh * kh, axis=-1, keepdims=True) + EPS))

    probs = [(ck, hh) for ck in range(nck) for hh in range(DN_HEADS)]
    gc_c, eg_c, egl_c, ge_c, beta_c, gct_c = [], [], [], [], [], []
    for ck in range(nck):
        g = chunk_rows(gc_col, ck)
        g_last = g[last:last + 1, :]
        gc_c.append(g)
        eg_c.append(jnp.exp(g))
        egl_c.append(jnp.exp(g_last - g))
        ge_c.append(jnp.exp(g_last))
        beta_c.append(chunk_rows(beta_col, ck))
        row = gc_t[0:DN_HEADS, ck * grp:(ck + 1) * grp]
        gct_c.append(row if grp == c else jnp.concatenate([row, jnp.zeros((DN_HEADS, c - grp), F32)], axis=1))
    q_p = [chunk_rows(qn[hh], ck) for ck, hh in probs]
    k_p = [chunk_rows(kn[hh], ck) for ck, hh in probs]
    v_p = [chunk_rows(qkv[:, 2 * w + hh * hd:2 * w + (hh + 1) * hd], ck) for ck, hh in probs]
    beta_p = [beta_c[ck][:, DN_HEADS + hh:DN_HEADS + hh + 1] for ck, hh in probs]
    eg_p = [eg_c[ck][:, hh:hh + 1] for ck, hh in probs]
    dec_p = [jnp.where(causal, jnp.exp(jnp.minimum(gc_c[ck][:, hh:hh + 1] - gct_c[ck][hh:hh + 1, :], 0.0)), 0.0)
             for ck, hh in probs]
    kb_p = [k * beta for k, beta in zip(k_p, beta_p)]
    kkqk_p = [_dot_nt(bf(jnp.concatenate([kb, q], axis=0)), bf(k)) for kb, q, k in zip(kb_p, q_p, k_p)]
    a_p = [jnp.where(strict, kkqk[:c] * dec, 0.0) for kkqk, dec in zip(kkqk_p, dec_p)]
    qk_p = [bf(kkqk[c:] * dec) for kkqk, dec in zip(kkqk_p, dec_p)]
    tinv_p = _unit_lower_inverses(a_p, c)
    rhs_p = [bf(jnp.concatenate([v * beta, kb * eg], axis=1)) for v, beta, kb, eg in zip(v_p, beta_p, kb_p, eg_p)]
    sol_p = [_dot(bf(t_inv), rhs) for t_inv, rhs in zip(tinv_p, rhs_p)]
    kcq_p = [bf(jnp.concatenate([sol[:, hd:], q * eg], axis=0)) for sol, q, eg in zip(sol_p, q_p, eg_p)]
    kd_p = [bf(k * egl_c[ck][:, hh:hh + 1]) for k, (ck, hh) in zip(k_p, probs)]

    o_p = [None] * len(probs)
    for ci in range(nc):
        sel = [(b, hh, (b * nc + ci) * DN_HEADS + hh) for b in range(bb) for hh in range(DN_HEADS)]
        s_old = [s_scr[b, hh] for b, hh, _ in sel]
        r = [_dot(kcq_p[p], bf(s)) for (_, _, p), s in zip(sel, s_old)]
        ub = [bf(sol_p[p][:, :hd] - rr[:c]) for (_, _, p), rr in zip(sel, r)]
        for (b, hh, p), rr, u, s in zip(sel, r, ub, s_old):
            o_p[p] = rr[c:] + _dot(qk_p[p], u)
            s_scr[b, hh] = s * ge_c[p // DN_HEADS][:, hh:hh + 1] + _dot_tn(kd_p[p], u)

    o = jnp.concatenate(
        [jnp.concatenate(o_p[ck * DN_HEADS:(ck + 1) * DN_HEADS], axis=1)[0:grp] for ck in range(nck)], axis=0)
    gate = proj[:, 3 * w:]
    ng = ng_ref[...]
    outs = []
    for hh in range(DN_HEADS):
        sl = slice(hh * hd, (hh + 1) * hd)
        outs.append(_rms(o[:, sl], ng) * _silu(gate[:, sl]))
    y = _dot(jnp.concatenate(outs, axis=1).astype(BF16), wout_ref[...])
    o_ref[...] = (x + _rms(y, gpost_ref[...])).reshape(bb, tl, d)

    @pl.when(t == pl.num_programs(1) - 1)
    def _():
        sout_ref[...] = s_scr[...]
        cout_ref[...] = tail


def _dn(x, s0, conv_buf, norm_pre, norm_post, wqkvg, wab, cw, alog, dtb, ng, wout, i, j, bb, tl):
    b, l, d = x.shape
    hd, nh, w = DN_HEAD_DIM, DN_HEADS, DN_WIDTH
    nhist = DN_CONV_W - 1
    x_spec = pl.BlockSpec((bb, tl, d), lambda bi, ti: (bi, ti, 0))
    s_in_spec = pl.BlockSpec((None, bb, nh, hd, hd), lambda bi, ti: (j, bi, 0, 0, 0))
    c_in_spec = pl.BlockSpec((None, bb, nhist, 3 * w), lambda bi, ti: (j, bi, 0, 0))
    return pl.pallas_call(
        functools.partial(_dn_kernel, bb=bb, tl=tl),
        grid=(b // bb, l // tl),
        in_specs=[x_spec, s_in_spec, c_in_spec, _fixed((1, d), i, 1), _fixed((1, d), i, 1),
                  _fixed((d, 4 * w), j), _fixed((d, LANES), j), _fixed((DN_CONV_W, 3 * w), j),
                  _fixed((LANES, 1), j), _fixed((LANES, 1), j), _fixed((1, hd), j), _fixed((w, d), j)],
        out_specs=[x_spec,
                   pl.BlockSpec((bb, nh, hd, hd), lambda bi, ti: (bi, 0, 0, 0)),
                   pl.BlockSpec((bb, nhist, 3 * w), lambda bi, ti: (bi, 0, 0))],
        out_shape=[jax.ShapeDtypeStruct((b, l, d), F32),
                   jax.ShapeDtypeStruct((b, nh, hd, hd), F32),
                   jax.ShapeDtypeStruct((b, nhist, 3 * w), F32)],
        scratch_shapes=[pltpu.VMEM((bb, DN_HIST_OFF + tl, 3 * w), F32), pltpu.VMEM((bb, nh, hd, hd), F32)],
        compiler_params=_params("parallel", "arbitrary"),
        name=f"deltanet_l{i}",
    )(x, s0, conv_buf, norm_pre, norm_post, wqkvg, wab, cw, alog, dtb, ng, wout)


def _tiles(b, l):
    if l >= 512:
        return 512, 1, 512, 256
    bb = max(1, min(b, LANES // l))
    return min(b * l, 512), bb, l, l


def _trunk(x, mem_k, mem_v, conv_bufs, dn_states, dn_bufs, P):
    b, l, d = x.shape
    depth = P["wq"].shape[0]
    tm, bb, tl, tl_dn = _tiles(b, l)
    ffn = lambda x, i, k, j: _ffn(x.reshape(b * l, d), P["norm_pre"], P["norm_post"], P["wg"], P["wu"], P["wd"],
                                  i, k, j, tm).reshape(b, l, d)
    new_conv, new_states, new_dn_bufs = [], [], []
    for i in range(depth):
        x = ffn(x, i, 0, 0)
        j = i // 2
        if i % 2 == 0:
            x, buf = _conv(x, conv_bufs, P["norm_pre"], P["norm_post"], P["ca_win"], P["ca_b_in"], P["ca_dw"],
                           P["ca_dw_b"], P["ca_ln_g"], P["ca_ln_b"], P["ca_wout"], P["ca_b_out"], i, j, bb, tl)
            new_conv.append(buf)
        else:
            x, s, buf = _dn(x, dn_states, dn_bufs, P["norm_pre"], P["norm_post"], P["dn_wqkvg"], P["dn_wab"],
                            P["dn_conv_w"], P["dn_alog"], P["dn_dtb"], P["dn_norm_g"], P["dn_wout"], i, j, bb, tl_dn)
            new_states.append(s)
            new_dn_bufs.append(buf)
        x = _xattn(x, mem_k, mem_v, P["norm_pre"], P["norm_post"], P["wq"], P["wo"], i, bb, tl)
        x = ffn(x, i, 3, 1)
    return x, jnp.stack(new_conv), jnp.stack(new_states), jnp.stack(new_dn_bufs)


def kernel(x_prompt, x_sample, mem_prompt, cache_conv_a, state_dn, cache_dn_conv, cache_mem_k, cache_mem_v,
           norm_pre, norm_post, ffn_w_gate, ffn_w_up, ffn_w_down, xa_wq, xa_wk, xa_wv, xa_wo,
           ca_w_in, ca_b_in, ca_dw, ca_dw_b, ca_ln_g, ca_ln_b, ca_w_out, ca_b_out,
           dn_w_in, dn_conv_w, dn_a_log, dn_dt_bias, dn_norm_g, dn_w_out):
    depth, _, d = norm_pre.shape
    bp, _, _ = x_prompt.shape
    bs = x_sample.shape[0]
    nm = mem_prompt.shape[1]
    n_conv, n_dn = ca_w_in.shape[0], dn_w_in.shape[0]
    w, nh = DN_WIDTH, DN_HEADS
    row = lambda p: p[..., None, :]
    col = lambda p: jnp.pad(p, ((0, 0), (0, LANES - nh)))[..., None]
    P = dict(
        norm_pre=row(norm_pre), norm_post=row(norm_post),
        wg=ffn_w_gate.astype(BF16), wu=ffn_w_up.astype(BF16), wd=ffn_w_down.astype(BF16),
        wq=xa_wq.astype(BF16), wo=xa_wo.astype(BF16),
        ca_win=ca_w_in.astype(BF16), ca_b_in=row(ca_b_in), ca_dw=ca_dw, ca_dw_b=row(ca_dw_b),
        ca_ln_g=row(ca_ln_g), ca_ln_b=row(ca_ln_b), ca_wout=ca_w_out.astype(BF16), ca_b_out=row(ca_b_out),
        dn_wqkvg=dn_w_in[:, :, :4 * w].astype(BF16),
        dn_wab=jnp.pad(dn_w_in[:, :, 4 * w:], ((0, 0), (0, 0), (0, LANES - 2 * nh))).astype(BF16),
        dn_conv_w=dn_conv_w, dn_alog=col(dn_a_log), dn_dtb=col(dn_dt_bias), dn_norm_g=row(dn_norm_g),
        dn_wout=dn_w_out.astype(BF16),
    )
    p_k, p_v = _memkv(mem_prompt.reshape(bp * nm, d), xa_wk.astype(BF16), xa_wv.astype(BF16), 512)
    p_k = p_k.reshape(depth, bp, nm, d)
    p_v = p_v.reshape(depth, bp, nm, d)
    zero_conv = jnp.zeros((n_conv, bp) + cache_conv_a.shape[2:], F32)
    zero_state = jnp.zeros((n_dn, bp) + state_dn.shape[2:], F32)
    zero_dn_conv = jnp.zeros((n_dn, bp) + cache_dn_conv.shape[2:], F32)
    y_p, p_conv_a, p_state_dn, p_dn_conv = _trunk(x_prompt, p_k, p_v, zero_conv, zero_state, zero_dn_conv, P)
    y_s, s_conv_a, s_state_dn, s_dn_conv = _trunk(
        x_sample, cache_mem_k.reshape(depth, bs, nm, d), cache_mem_v.reshape(depth, bs, nm, d),
        cache_conv_a, state_dn, cache_dn_conv, P)
    kv_shape = (depth, bp) + cache_mem_k.shape[2:]
    return (y_p, y_s, p_conv_a, p_state_dn, p_dn_conv, p_k.reshape(kv_shape), p_v.reshape(kv_shape),
            s_conv_a, s_state_dn, s_dn_conv)
```

```python
import functools

import jax
import jax.numpy as jnp
from jax import lax
from jax.experimental import pallas as pl
from jax.experimental.pallas import tpu as pltpu

F32 = jnp.float32
BF16 = jnp.bfloat16
EPS = 1e-6

DN_HEADS = 8
DN_HEAD_DIM = 128
DN_WIDTH = DN_HEADS * DN_HEAD_DIM
DN_CHUNK = 64
DN_INV_BLOCK = 16
DN_CONV_W = 4
DN_HIST_OFF = 8
CA_CONV_W = 31
CA_HIST_OFF = 32
CA_ROW_BLOCK = 64
XA_HEADS = 4
XA_PART_ROWS = 256
FFN_PART_ROWS = 256
MEMKV_BATCH_ROWS = 2
LANES = 128
SUBLANES = 8

VMEM_LIMIT_BYTES = 56 * 1024 * 1024


def _rms(x, g):
    return x * lax.rsqrt(jnp.mean(x * x, axis=-1, keepdims=True) + EPS) * g


def _silu(x):
    return x * jax.nn.sigmoid(x)


def _dot(a, b):
    return jnp.dot(a, b, preferred_element_type=F32)


def _dot_nt(a, b):
    return lax.dot_general(a, b, (((1,), (1,)), ((), ())), preferred_element_type=F32)


def _dot_tn(a, b):
    return lax.dot_general(a, b, (((0,), (0,)), ((), ())), preferred_element_type=F32)


def _fixed(tail, *lead):
    n_tail = len(tail)
    idx = tuple(lead) + (0,) * n_tail
    return pl.BlockSpec((None,) * len(lead) + tuple(tail), lambda *_: idx, pipeline_mode=pl.Buffered(1))


def _params(*sem):
    return pltpu.CompilerParams(dimension_semantics=sem, vmem_limit_bytes=VMEM_LIMIT_BYTES)


def _ffn_kernel(x_ref, gpre_ref, gpost_ref, wg_ref, wu_ref, wd_ref, o_ref):
    tm = x_ref.shape[0]
    rows = min(FFN_PART_ROWS, tm)
    for r in range(tm // rows):
        rs = slice(r * rows, (r + 1) * rows)
        x = x_ref[rs, :]
        h = _rms(x, gpre_ref[...]).astype(BF16)
        g = _dot(h, wg_ref[...])
        u = _dot(h, wu_ref[...])
        a = (_silu(g) * u).astype(BF16)
        f = _dot(a, wd_ref[...])
        o_ref[rs, :] = x + 0.5 * _rms(f, gpost_ref[...])


def _ffn(x2d, norm_pre, norm_post, wg, wu, wd, i, k, j, tm):
    n, d = x2d.shape
    dff = wg.shape[-1]
    return pl.pallas_call(
        _ffn_kernel,
        grid=(n // tm,),
        in_specs=[
            pl.BlockSpec((tm, d), lambda m: (m, 0)),
            _fixed((1, d), i, k),
            _fixed((1, d), i, k),
            _fixed((d, dff), i, j),
            _fixed((d, dff), i, j),
            _fixed((dff, d), i, j),
        ],
        out_specs=pl.BlockSpec((tm, d), lambda m: (m, 0)),
        out_shape=jax.ShapeDtypeStruct((n, d), F32),
        compiler_params=_params("parallel"),
        name=f"ffn_l{i}_{j}",
    )(x2d, norm_pre, norm_post, wg, wu, wd)


def _memkv_kernel(m_ref, wk_ref, wv_ref, k_ref, v_ref, kb_ref, vb_ref):
    tb, nm, nh, dh = k_ref.shape
    m = m_ref[...].astype(BF16)
    for w_ref, o_ref, ob_ref in ((wk_ref, k_ref, kb_ref), (wv_ref, v_ref, vb_ref)):
        y3 = _dot(m, w_ref[...]).reshape(tb, nm, nh * dh)
        ob_ref[...] = y3.astype(BF16)
        for hh in range(nh):
            o_ref[:, :, hh, :] = y3[:, :, hh * dh:(hh + 1) * dh]


def _memkv(mem, wk, wv, tb):
    b, nm, d = mem.shape
    depth = wk.shape[0]
    nh, dh = XA_HEADS, d // XA_HEADS
    w_spec = pl.BlockSpec((None, d, d), lambda l, m: (l, 0, 0))
    o_spec = pl.BlockSpec((None, tb, nm, nh, dh), lambda l, m: (l, m, 0, 0, 0))
    ob_spec = pl.BlockSpec((None, tb, nm, d), lambda l, m: (l, m, 0, 0))
    return pl.pallas_call(
        _memkv_kernel,
        grid=(depth, b // tb),
        in_specs=[pl.BlockSpec((tb * nm, d), lambda l, m: (m, 0)), w_spec, w_spec],
        out_specs=[o_spec, o_spec, ob_spec, ob_spec],
        out_shape=[jax.ShapeDtypeStruct((depth, b, nm, nh, dh), F32)] * 2
        + [jax.ShapeDtypeStruct((depth, b, nm, d), BF16)] * 2,
        compiler_params=_params("parallel", "parallel"),
        name="mem_kv",
    )(mem.reshape(b * nm, d), wk, wv)


def _xattn_kernel(x_ref, k_ref, v_ref, gpre_ref, gpost_ref, wq_ref, wo_ref, o_ref, *, bb, tl):
    d = x_ref.shape[-1]
    dh = d // XA_HEADS
    rows = min(XA_PART_ROWS, tl)
    parts = tl // rows

    def head(ref, b, hh):
        blk = ref[b, :, hh, :] if len(ref.shape) == 4 else ref[b, :, hh * dh:(hh + 1) * dh]
        return blk.astype(BF16)

    kh = [[head(k_ref, b, hh) for hh in range(XA_HEADS)] for b in range(bb)]
    vh = [[head(v_ref, b, hh) for hh in range(XA_HEADS)] for b in range(bb)]

    def load(r):
        x = x_ref[:, r * rows:(r + 1) * rows, :].reshape(bb * rows, d)
        return x, _dot(_rms(x, gpre_ref[...]).astype(BF16), wq_ref[...]).astype(BF16)

    def scores(q):
        return [[_dot_nt(q[b * rows:(b + 1) * rows, hh * dh:(hh + 1) * dh], kh[b][hh]) * (dh ** -0.5)
                 for hh in range(XA_HEADS)] for b in range(bb)]

    def softmax(s):
        p = jnp.exp(s - jnp.max(s, axis=-1, keepdims=True))
        return (p * (1.0 / jnp.sum(p, axis=-1, keepdims=True))).astype(BF16)

    def attend(s):
        outs = [jnp.concatenate([_dot(softmax(s[b][hh]), vh[b][hh]) for hh in range(XA_HEADS)], axis=1)
                for b in range(bb)]
        return (jnp.concatenate(outs, axis=0) if bb > 1 else outs[0]).astype(BF16)

    x, q = load(0)
    s = scores(q)
    for r in range(parts):
        if r + 1 < parts:
            x_next, q = load(r + 1)
        o = attend(s)
        if r + 1 < parts:
            s = scores(q)
        a = _dot(o, wo_ref[...])
        o_ref[:, r * rows:(r + 1) * rows, :] = (x + _rms(a, gpost_ref[...])).reshape(bb, rows, d)
        if r + 1 < parts:
            x = x_next


def _xattn(x, mem_k, mem_v, norm_pre, norm_post, wq, wo, i, bb, tl):
    b, l, d = x.shape
    x_spec = pl.BlockSpec((bb, tl, d), lambda bi, ti: (bi, ti, 0))
    kv_tail = mem_k.shape[2:]
    kv_spec = pl.BlockSpec((None, bb) + kv_tail, lambda bi, ti: (i, bi) + (0,) * len(kv_tail))
    return pl.pallas_call(
        functools.partial(_xattn_kernel, bb=bb, tl=tl),
        grid=(b // bb, l // tl),
        in_specs=[x_spec, kv_spec, kv_spec, _fixed((1, d), i, 2), _fixed((1, d), i, 2),
                  _fixed((d, d), i), _fixed((d, d), i)],
        out_specs=x_spec,
        out_shape=jax.ShapeDtypeStruct((b, l, d), F32),
        compiler_params=_params("parallel", "parallel"),
        name=f"xattn_l{i}",
    )(x, mem_k, mem_v, norm_pre, norm_post, wq, wo)


def _conv_kernel(x_ref, cbuf_ref, gpre_ref, gpost_ref, win_ref, bin_ref, dw_ref, dwb_ref, lng_ref, lnb_ref,
                 wout_ref, bout_ref, o_ref, cout_ref, hist_scr, c_scr, *, bb, tl):
    d = x_ref.shape[-1]
    m = bb * tl
    hist0 = CA_HIST_OFF - (CA_CONV_W - 1)
    t = pl.program_id(1)

    @pl.when(t == 0)
    def _():
        hist_scr[:, 0, 0:SUBLANES, :] = jnp.zeros((bb, SUBLANES, d), F32)
        hist_scr[:, 0, hist0:CA_HIST_OFF, :] = cbuf_ref[...]

    x = x_ref[...].reshape(m, d)
    h = _rms(x, gpre_ref[...]).astype(BF16)
    u = _dot(h, win_ref[...]) + bin_ref[...]
    glu = u[:, :d] * jax.nn.sigmoid(u[:, d:])
    hist_scr[:, 0, CA_HIST_OFF:CA_HIST_OFF + tl, :] = glu.reshape(bb, tl, d)
    n_shift = tl + CA_HIST_OFF - SUBLANES
    for s in range(1, SUBLANES):
        hist_scr[:, s, 0:n_shift, :] = hist_scr[:, 0, s:s + n_shift, :]

    rb = min(CA_ROW_BLOCK, tl)
    groups = rb // SUBLANES
    for b in range(bb):
        for lc in range(d // LANES):
            ls = slice(lc * LANES, (lc + 1) * LANES)
            taps = [jnp.broadcast_to(dw_ref[j:j + 1, ls], (SUBLANES, LANES)) for j in range(CA_CONV_W)]
            bias = jnp.broadcast_to(dwb_ref[:, ls], (SUBLANES, LANES))

            def row_block(r, carry, b=b, ls=ls, taps=taps, bias=bias):
                r0 = pl.multiple_of(r * rb, rb)
                for g in range(groups):
                    acc = bias
                    for j in range(CA_CONV_W):
                        s, a = (hist0 + j) % SUBLANES, (hist0 + j) // SUBLANES
                        acc = acc + taps[j] * hist_scr[b, s, pl.ds(r0 + (a + g) * SUBLANES, SUBLANES), ls]
                    c_scr[pl.ds(b * tl + r0 + g * SUBLANES, SUBLANES), ls] = acc
                return carry
            lax.fori_loop(0, tl // rb, row_block, 0)

    c = c_scr[...]
    mu = jnp.mean(c, axis=-1, keepdims=True)
    cc = c - mu
    ln = cc * lax.rsqrt(jnp.mean(cc * cc, axis=-1, keepdims=True) + EPS) * lng_ref[...] + lnb_ref[...]
    y = _dot(_silu(ln).astype(BF16), wout_ref[...]) + bout_ref[...]
    o_ref[...] = (x + _rms(y, gpost_ref[...])).reshape(bb, tl, d)

    tail = hist_scr[:, 0, tl + hist0:tl + CA_HIST_OFF, :]
    hist_scr[:, 0, hist0:CA_HIST_OFF, :] = tail

    @pl.when(t == pl.num_programs(1) - 1)
    def _():
        cout_ref[...] = tail


def _conv(x, conv_buf, norm_pre, norm_post, win, b_in, dw, dw_b, ln_g, ln_b, wout, b_out, i, j, bb, tl):
    b, l, d = x.shape
    nh = CA_CONV_W - 1
    x_spec = pl.BlockSpec((bb, tl, d), lambda bi, ti: (bi, ti, 0))
    c_spec = pl.BlockSpec((None, bb, nh, d), lambda bi, ti: (j, bi, 0, 0))
    return pl.pallas_call(
        functools.partial(_conv_kernel, bb=bb, tl=tl),
        grid=(b // bb, l // tl),
        in_specs=[x_spec, c_spec, _fixed((1, d), i, 1), _fixed((1, d), i, 1),
                  _fixed((d, 2 * d), j), _fixed((1, 2 * d), j), _fixed((CA_CONV_W, d), j), _fixed((1, d), j),
                  _fixed((1, d), j), _fixed((1, d), j), _fixed((d, d), j), _fixed((1, d), j)],
        out_specs=[x_spec, pl.BlockSpec((bb, nh, d), lambda bi, ti: (bi, 0, 0))],
        out_shape=[jax.ShapeDtypeStruct((b, l, d), F32), jax.ShapeDtypeStruct((b, nh, d), F32)],
        scratch_shapes=[pltpu.VMEM((bb, SUBLANES, CA_HIST_OFF + tl, d), F32), pltpu.VMEM((bb * tl, d), F32)],
        compiler_params=_params("parallel", "arbitrary"),
        name=f"conv_l{i}",
    )(x, conv_buf, norm_pre, norm_post, win, b_in, dw, dw_b, ln_g, ln_b, wout, b_out)


def _unit_lower_inverses(ns, c):
    ri = lax.broadcasted_iota(jnp.int32, (c, c), 0)
    ci = lax.broadcasted_iota(jnp.int32, (c, c), 1)
    shift = DN_INV_BLOCK.bit_length() - 1
    eye = (ri == ci).astype(F32)
    same = (ri >> shift) == (ci >> shift)
    mm = lambda a, b: _dot(a.astype(BF16), b.astype(BF16))
    xs = [jnp.where(same, n, 0.0) for n in ns]
    ps = [eye - x for x in xs]
    size = 2
    while size < DN_INV_BLOCK:
        xs = [mm(x, x) for x in xs]
        ps = [p + px for p, px in zip(ps, [mm(p, x) for p, x in zip(ps, xs)])]
        size *= 2
    size = DN_INV_BLOCK
    while size < c:
        s = size.bit_length() - 1
        rblk = ri >> s
        off = ((rblk & 1) == 1) & ((ci >> s) == rblk - 1)
        aps = [mm(jnp.where(off, n, 0.0), p) for n, p in zip(ns, ps)]
        ps = [p - pap for p, pap in zip(ps, [mm(p, ap) for p, ap in zip(ps, aps)])]
        size *= 2
    return ps


def _dn_kernel(x_ref, s0_ref, cbuf_ref, gpre_ref, gpost_ref, wqkvg_ref, wab_ref, cw_ref, alog_ref, dtb_ref,
               ng_ref, wout_ref, o_ref, sout_ref, cout_ref, hist_scr, s_scr, *, bb, tl):
    d = x_ref.shape[-1]
    hd = DN_HEAD_DIM
    w = DN_WIDTH
    m = bb * tl
    c = DN_CHUNK
    nc = -(-tl // c)
    grp = min(c, tl)
    nck = bb * nc
    hist0 = DN_HIST_OFF - (DN_CONV_W - 1)
    t = pl.program_id(1)

    @pl.when(t == 0)
    def _():
        s_scr[...] = s0_ref[...]
        hist_scr[:, hist0:DN_HIST_OFF, :] = cbuf_ref[...]

    x = x_ref[...].reshape(m, d)
    h = _rms(x, gpre_ref[...]).astype(BF16)
    proj = _dot(h, wqkvg_ref[...])
    ab = _dot(h, wab_ref[...])
    hist_scr[:, DN_HIST_OFF:DN_HIST_OFF + tl, :] = proj[:, :3 * w].reshape(bb, tl, 3 * w)
    conv = cw_ref[0:1, :] * hist_scr[:, hist0:hist0 + tl, :]
    for j in range(1, DN_CONV_W):
        conv = conv + cw_ref[j:j + 1, :] * hist_scr[:, hist0 + j:hist0 + j + tl, :]
    qkv = _silu(conv).reshape(m, 3 * w)
    tail = hist_scr[:, tl + hist0:tl + DN_HIST_OFF, :]
    hist_scr[:, hist0:DN_HIST_OFF, :] = tail

    abt = ab.T
    g_t = -jnp.exp(alog_ref[...]) * jax.nn.softplus(abt + dtb_ref[...])
    pos = lax.broadcasted_iota(jnp.int32, g_t.shape, 1) & (grp - 1)
    gc_t = g_t
    step = 1
    while step < grp:
        gc_t = gc_t + jnp.where(pos >= step, pltpu.roll(gc_t, step, axis=1), 0.0)
        step *= 2
    gc_col = gc_t.T
    beta_col = jax.nn.sigmoid(ab)

    ri = lax.broadcasted_iota(jnp.int32, (c, c), 0)
    ci_ = lax.broadcasted_iota(jnp.int32, (c, c), 1)
    causal = ri >= ci_
    strict = ri > ci_
    last = grp - 1
    bf = lambda a: a.astype(BF16)

    def chunk_rows(a, ck):
        blk = a[ck * grp:(ck + 1) * grp]
        if grp == c:
            return blk
        return jnp.concatenate([blk, jnp.zeros((c - grp, a.shape[1]), a.dtype)], axis=0)

    qn, kn = [], []
    for hh in range(DN_HEADS):
        qh = qkv[:, hh * hd:(hh + 1) * hd]
        kh = qkv[:, w + hh * hd:w + (hh + 1) * hd]
        qn.append(qh * lax.rsqrt(jnp.sum(qh * qh, axis=-1, keepdims=True) + EPS) * (hd ** -0.5))
        kn.append(kh * lax.rsqrt(jnp.sum(kh * kh, axis=-1, keepdims=True) + EPS))

    probs = [(ck, hh) for ck in range(nck) for hh in range(DN_HEADS)]
    gc_c, eg_c, egl_c, ge_c, beta_c, gct_c = [], [], [], [], [], []
    for ck in range(nck):
        g = chunk_rows(gc_col, ck)
        g_last = g[last:last + 1, :]
        gc_c.append(g)
        eg_c.append(jnp.exp(g))
        egl_c.append(jnp.exp(g_last - g))
        ge_c.append(jnp.exp(g_last))
        beta_c.append(chunk_rows(beta_col, ck))
        row = gc_t[0:DN_HEADS, ck * grp:(ck + 1) * grp]
        gct_c.append(row if grp == c else jnp.concatenate([row, jnp.zeros((DN_HEADS, c - grp), F32)], axis=1))
    q_p = [chunk_rows(qn[hh], ck) for ck, hh in probs]
    k_p = [chunk_rows(kn[hh], ck) for ck, hh in probs]
    v_p = [chunk_rows(qkv[:, 2 * w + hh * hd:2 * w + (hh + 1) * hd], ck) for ck, hh in probs]
    beta_p = [beta_c[ck][:, DN_HEADS + hh:DN_HEADS + hh + 1] for ck, hh in probs]
    eg_p = [eg_c[ck][:, hh:hh + 1] for ck, hh in probs]
    dec_p = [jnp.where(causal, jnp.exp(jnp.minimum(gc_c[ck][:, hh:hh + 1] - gct_c[ck][hh:hh + 1, :], 0.0)), 0.0)
             for ck, hh in probs]
    kb_p = [k * beta for k, beta in zip(k_p, beta_p)]
    kkqk_p = [_dot_nt(bf(jnp.concatenate([kb, q], axis=0)), bf(k)) for kb, q, k in zip(kb_p, q_p, k_p)]
    a_p = [jnp.where(strict, kkqk[:c] * dec, 0.0) for kkqk, dec in zip(kkqk_p, dec_p)]
    qk_p = [bf(kkqk[c:] * dec) for kkqk, dec in zip(kkqk_p, dec_p)]
    tinv_p = _unit_lower_inverses(a_p, c)
    rhs_p = [bf(jnp.concatenate([v * beta, kb * eg], axis=1)) for v, beta, kb, eg in zip(v_p, beta_p, kb_p, eg_p)]
    sol_p = [_dot(bf(t_inv), rhs) for t_inv, rhs in zip(tinv_p, rhs_p)]
    kcq_p = [bf(jnp.concatenate([sol[:, hd:], q * eg], axis=0)) for sol, q, eg in zip(sol_p, q_p, eg_p)]
    kd_p = [bf(k * egl_c[ck][:, hh:hh + 1]) for k, (ck, hh) in zip(k_p, probs)]

    o_p = [None] * len(probs)
    for ci in range(nc):
        sel = [(b, hh, (b * nc + ci) * DN_HEADS + hh) for b in range(bb) for hh in range(DN_HEADS)]
        s_old = [s_scr[b, hh] for b, hh, _ in sel]
        r = [_dot(kcq_p[p], bf(s)) for (_, _, p), s in zip(sel, s_old)]
        ub = [bf(sol_p[p][:, :hd] - rr[:c]) for (_, _, p), rr in zip(sel, r)]
        for (b, hh, p), rr, u, s in zip(sel, r, ub, s_old):
            o_p[p] = rr[c:] + _dot(qk_p[p], u)
            s_scr[b, hh] = s * ge_c[p // DN_HEADS][:, hh:hh + 1] + _dot_tn(kd_p[p], u)

    o = jnp.concatenate(
        [jnp.concatenate(o_p[ck * DN_HEADS:(ck + 1) * DN_HEADS], axis=1)[0:grp] for ck in range(nck)], axis=0)
    gate = proj[:, 3 * w:]
    ng = ng_ref[...]
    outs = []
    for hh in range(DN_HEADS):
        sl = slice(hh * hd, (hh + 1) * hd)
        outs.append(_rms(o[:, sl], ng) * _silu(gate[:, sl]))
    y = _dot(jnp.concatenate(outs, axis=1).astype(BF16), wout_ref[...])
    o_ref[...] = (x + _rms(y, gpost_ref[...])).reshape(bb, tl, d)

    @pl.when(t == pl.num_programs(1) - 1)
    def _():
        sout_ref[...] = s_scr[...]
        cout_ref[...] = tail


def _dn(x, s0, conv_buf, norm_pre, norm_post, wqkvg, wab, cw, alog, dtb, ng, wout, i, j, bb, tl):
    b, l, d = x.shape
    hd, nh, w = DN_HEAD_DIM, DN_HEADS, DN_WIDTH
    nhist = DN_CONV_W - 1
    x_spec = pl.BlockSpec((bb, tl, d), lambda bi, ti: (bi, ti, 0))
    s_in_spec = pl.BlockSpec((None, bb, nh, hd, hd), lambda bi, ti: (j, bi, 0, 0, 0))
    c_in_spec = pl.BlockSpec((None, bb, nhist, 3 * w), lambda bi, ti: (j, bi, 0, 0))
    return pl.pallas_call(
        functools.partial(_dn_kernel, bb=bb, tl=tl),
        grid=(b // bb, l // tl),
        in_specs=[x_spec, s_in_spec, c_in_spec, _fixed((1, d), i, 1), _fixed((1, d), i, 1),
                  _fixed((d, 4 * w), j), _fixed((d, LANES), j), _fixed((DN_CONV_W, 3 * w), j),
                  _fixed((LANES, 1), j), _fixed((LANES, 1), j), _fixed((1, hd), j), _fixed((w, d), j)],
        out_specs=[x_spec,
                   pl.BlockSpec((bb, nh, hd, hd), lambda bi, ti: (bi, 0, 0, 0)),
                   pl.BlockSpec((bb, nhist, 3 * w), lambda bi, ti: (bi, 0, 0))],
        out_shape=[jax.ShapeDtypeStruct((b, l, d), F32),
                   jax.ShapeDtypeStruct((b, nh, hd, hd), F32),
                   jax.ShapeDtypeStruct((b, nhist, 3 * w), F32)],
        scratch_shapes=[pltpu.VMEM((bb, DN_HIST_OFF + tl, 3 * w), F32), pltpu.VMEM((bb, nh, hd, hd), F32)],
        compiler_params=_params("parallel", "arbitrary"),
        name=f"deltanet_l{i}",
    )(x, s0, conv_buf, norm_pre, norm_post, wqkvg, wab, cw, alog, dtb, ng, wout)


def _tiles(b, l):
    if l >= 512:
        return 1024, 1, 512, 2, 128
    bb = max(1, min(b, LANES // l))
    return min(b * l, 512), bb, l, bb, l


def _trunk(x, mem_k, mem_v, conv_bufs, dn_states, dn_bufs, P):
    b, l, d = x.shape
    depth = P["wq"].shape[0]
    tm, bb, tl, bb_dn, tl_dn = _tiles(b, l)
    ffn = lambda x, i, k, j: _ffn(x.reshape(b * l, d), P["norm_pre"], P["norm_post"], P["wg"], P["wu"], P["wd"],
                                  i, k, j, tm).reshape(b, l, d)
    new_conv, new_states, new_dn_bufs = [], [], []
    for i in range(depth):
        x = ffn(x, i, 0, 0)
        j = i // 2
        if i % 2 == 0:
            x, buf = _conv(x, conv_bufs, P["norm_pre"], P["norm_post"], P["ca_win"], P["ca_b_in"], P["ca_dw"],
                           P["ca_dw_b"], P["ca_ln_g"], P["ca_ln_b"], P["ca_wout"], P["ca_b_out"], i, j, bb, tl)
            new_conv.append(buf)
        else:
            x, s, buf = _dn(x, dn_states, dn_bufs, P["norm_pre"], P["norm_post"], P["dn_wqkvg"], P["dn_wab"],
                            P["dn_conv_w"], P["dn_alog"], P["dn_dtb"], P["dn_norm_g"], P["dn_wout"], i, j,
                            bb_dn, tl_dn)
            new_states.append(s)
            new_dn_bufs.append(buf)
        x = _xattn(x, mem_k, mem_v, P["norm_pre"], P["norm_post"], P["wq"], P["wo"], i, bb, tl)
        x = ffn(x, i, 3, 1)
    return x, jnp.stack(new_conv), jnp.stack(new_states), jnp.stack(new_dn_bufs)


def _prep(norm_pre, norm_post, ffn_w_gate, ffn_w_up, ffn_w_down, xa_wq, xa_wo,
          ca_w_in, ca_b_in, ca_dw, ca_dw_b, ca_ln_g, ca_ln_b, ca_w_out, ca_b_out,
          dn_w_in, dn_conv_w, dn_a_log, dn_dt_bias, dn_norm_g, dn_w_out):
    w, nh = DN_WIDTH, DN_HEADS
    row = lambda p: p[..., None, :]
    col = lambda p: jnp.pad(p, ((0, 0), (0, LANES - nh)))[..., None]
    return dict(
        norm_pre=row(norm_pre), norm_post=row(norm_post),
        wg=ffn_w_gate.astype(BF16), wu=ffn_w_up.astype(BF16), wd=ffn_w_down.astype(BF16),
        wq=xa_wq.astype(BF16), wo=xa_wo.astype(BF16),
        ca_win=ca_w_in.astype(BF16), ca_b_in=row(ca_b_in), ca_dw=ca_dw, ca_dw_b=row(ca_dw_b),
        ca_ln_g=row(ca_ln_g), ca_ln_b=row(ca_ln_b), ca_wout=ca_w_out.astype(BF16), ca_b_out=row(ca_b_out),
        dn_wqkvg=dn_w_in[:, :, :4 * w].astype(BF16),
        dn_wab=jnp.pad(dn_w_in[:, :, 4 * w:], ((0, 0), (0, 0), (0, LANES - 2 * nh))).astype(BF16),
        dn_conv_w=dn_conv_w, dn_alog=col(dn_a_log), dn_dtb=col(dn_dt_bias), dn_norm_g=row(dn_norm_g),
        dn_wout=dn_w_out.astype(BF16),
    )


def kernel(x_prompt, x_sample, mem_prompt, cache_conv_a, state_dn, cache_dn_conv, cache_mem_k, cache_mem_v,
           norm_pre, norm_post, ffn_w_gate, ffn_w_up, ffn_w_down, xa_wq, xa_wk, xa_wv, xa_wo,
           ca_w_in, ca_b_in, ca_dw, ca_dw_b, ca_ln_g, ca_ln_b, ca_w_out, ca_b_out,
           dn_w_in, dn_conv_w, dn_a_log, dn_dt_bias, dn_norm_g, dn_w_out):
    bp = x_prompt.shape[0]
    n_conv, n_dn = ca_w_in.shape[0], dn_w_in.shape[0]
    P = _prep(norm_pre, norm_post, ffn_w_gate, ffn_w_up, ffn_w_down, xa_wq, xa_wo,
              ca_w_in, ca_b_in, ca_dw, ca_dw_b, ca_ln_g, ca_ln_b, ca_w_out, ca_b_out,
              dn_w_in, dn_conv_w, dn_a_log, dn_dt_bias, dn_norm_g, dn_w_out)
    p_k, p_v, p_k16, p_v16 = _memkv(mem_prompt, xa_wk.astype(BF16), xa_wv.astype(BF16), MEMKV_BATCH_ROWS)
    zero_conv = jnp.zeros((n_conv, bp) + cache_conv_a.shape[2:], F32)
    zero_state = jnp.zeros((n_dn, bp) + state_dn.shape[2:], F32)
    zero_dn_conv = jnp.zeros((n_dn, bp) + cache_dn_conv.shape[2:], F32)
    y_p, p_conv_a, p_state_dn, p_dn_conv = _trunk(x_prompt, p_k16, p_v16, zero_conv, zero_state, zero_dn_conv, P)
    y_s, s_conv_a, s_state_dn, s_dn_conv = _trunk(
        x_sample, cache_mem_k, cache_mem_v, cache_conv_a, state_dn, cache_dn_conv, P)
    return (y_p, y_s, p_conv_a, p_state_dn, p_dn_conv, p_k, p_v, s_conv_a, s_state_dn, s_dn_conv)
```

```python
import functools

import jax
import jax.numpy as jnp
from jax import lax
from jax.experimental import pallas as pl
from jax.experimental.pallas import tpu as pltpu

F32 = jnp.float32
BF16 = jnp.bfloat16
EPS = 1e-6

DN_HEADS = 8
DN_HEAD_DIM = 128
DN_WIDTH = DN_HEADS * DN_HEAD_DIM
DN_CHUNK = 64
DN_INV_BLOCK = 16
DN_CONV_W = 4
DN_HIST_OFF = 8
CA_CONV_W = 31
CA_HIST_OFF = 32
CA_ROW_BLOCK = 64
XA_HEADS = 4
XA_PART_ROWS = 256
FFN_PART_ROWS = 256
MEMKV_BATCH_ROWS = 2
LANES = 128
SUBLANES = 8

VMEM_LIMIT_BYTES = 56 * 1024 * 1024


def _rms(x, g):
    return x * lax.rsqrt(jnp.mean(x * x, axis=-1, keepdims=True) + EPS) * g


def _silu(x):
    return x * jax.nn.sigmoid(x)


def _dot(a, b):
    return jnp.dot(a, b, preferred_element_type=F32)


def _dot_nt(a, b):
    return lax.dot_general(a, b, (((1,), (1,)), ((), ())), preferred_element_type=F32)


def _dot_tn(a, b):
    return lax.dot_general(a, b, (((0,), (0,)), ((), ())), preferred_element_type=F32)


def _fixed(tail, *lead):
    n_tail = len(tail)
    idx = tuple(lead) + (0,) * n_tail
    return pl.BlockSpec((None,) * len(lead) + tuple(tail), lambda *_: idx, pipeline_mode=pl.Buffered(1))


def _params(*sem):
    return pltpu.CompilerParams(dimension_semantics=sem, vmem_limit_bytes=VMEM_LIMIT_BYTES)


def _ffn_kernel(x_ref, gpre_ref, gpost_ref, wg_ref, wu_ref, wd_ref, o_ref):
    tm = x_ref.shape[0]
    rows = min(FFN_PART_ROWS, tm)
    for r in range(tm // rows):
        rs = slice(r * rows, (r + 1) * rows)
        x = x_ref[rs, :]
        h = _rms(x, gpre_ref[...]).astype(BF16)
        g = _dot(h, wg_ref[...])
        u = _dot(h, wu_ref[...])
        a = (_silu(g) * u).astype(BF16)
        f = _dot(a, wd_ref[...])
        o_ref[rs, :] = x + 0.5 * _rms(f, gpost_ref[...])


def _ffn(x2d, norm_pre, norm_post, wg, wu, wd, i, k, j, tm):
    n, d = x2d.shape
    dff = wg.shape[-1]
    return pl.pallas_call(
        _ffn_kernel,
        grid=(n // tm,),
        in_specs=[
            pl.BlockSpec((tm, d), lambda m: (m, 0)),
            _fixed((1, d), i, k),
            _fixed((1, d), i, k),
            _fixed((d, dff), i, j),
            _fixed((d, dff), i, j),
            _fixed((dff, d), i, j),
        ],
        out_specs=pl.BlockSpec((tm, d), lambda m: (m, 0)),
        out_shape=jax.ShapeDtypeStruct((n, d), F32),
        compiler_params=_params("parallel"),
        name=f"ffn_l{i}_{j}",
    )(x2d, norm_pre, norm_post, wg, wu, wd)


def _memkv_kernel(m_ref, wk_ref, wv_ref, k_ref, v_ref, kb_ref, vb_ref):
    tb, nm, nh, dh = k_ref.shape
    m = m_ref[...].astype(BF16)
    for w_ref, o_ref, ob_ref in ((wk_ref, k_ref, kb_ref), (wv_ref, v_ref, vb_ref)):
        y3 = _dot(m, w_ref[...]).reshape(tb, nm, nh * dh)
        ob_ref[...] = y3.astype(BF16)
        for hh in range(nh):
            o_ref[:, :, hh, :] = y3[:, :, hh * dh:(hh + 1) * dh]


def _memkv(mem, wk, wv, tb):
    b, nm, d = mem.shape
    depth = wk.shape[0]
    nh, dh = XA_HEADS, d // XA_HEADS
    w_spec = pl.BlockSpec((None, d, d), lambda l, m: (l, 0, 0))
    o_spec = pl.BlockSpec((None, tb, nm, nh, dh), lambda l, m: (l, m, 0, 0, 0))
    ob_spec = pl.BlockSpec((None, tb, nm, d), lambda l, m: (l, m, 0, 0))
    return pl.pallas_call(
        _memkv_kernel,
        grid=(depth, b // tb),
        in_specs=[pl.BlockSpec((tb * nm, d), lambda l, m: (m, 0)), w_spec, w_spec],
        out_specs=[o_spec, o_spec, ob_spec, ob_spec],
        out_shape=[jax.ShapeDtypeStruct((depth, b, nm, nh, dh), F32)] * 2
        + [jax.ShapeDtypeStruct((depth, b, nm, d), BF16)] * 2,
        compiler_params=_params("parallel", "parallel"),
        name="mem_kv",
    )(mem.reshape(b * nm, d), wk, wv)


def _xattn_kernel(x_ref, k_ref, v_ref, gpre_ref, gpost_ref, wq_ref, wo_ref, o_ref, *, bb, tl):
    d = x_ref.shape[-1]
    dh = d // XA_HEADS
    rows = min(XA_PART_ROWS, tl)
    parts = tl // rows
    cache_layout = len(k_ref.shape) == 4
    nm = k_ref.shape[1]
    hs = [slice(hh * dh, (hh + 1) * dh) for hh in range(XA_HEADS)]

    if cache_layout:
        kh = [k_ref[b].reshape(nm * XA_HEADS, dh).astype(BF16) for b in range(bb)]
        vh = [v_ref[b].reshape(nm * XA_HEADS, dh).astype(BF16) for b in range(bb)]
        shape = (XA_HEADS * rows, nm * XA_HEADS)
        same_head = (lax.broadcasted_iota(jnp.int32, shape, 1) & (XA_HEADS - 1)) == (
            lax.broadcasted_iota(jnp.int32, shape, 0) // rows)
    else:
        kh = [[k_ref[b, :, sl] for sl in hs] for b in range(bb)]
        vh = [[v_ref[b, :, sl] for sl in hs] for b in range(bb)]

    def load(r):
        x = x_ref[:, r * rows:(r + 1) * rows, :].reshape(bb * rows, d)
        return x, _dot(_rms(x, gpre_ref[...]).astype(BF16), wq_ref[...]).astype(BF16)

    def scores(q):
        if cache_layout:
            out = []
            for b in range(bb):
                qb = q[b * rows:(b + 1) * rows]
                s = _dot_nt(jnp.concatenate([qb[:, sl] for sl in hs], axis=0), kh[b]) * (dh ** -0.5)
                out.append([jnp.where(same_head, s, -jnp.inf)])
            return out
        return [[_dot_nt(q[b * rows:(b + 1) * rows, sl], kh[b][hh]) * (dh ** -0.5) for hh, sl in enumerate(hs)]
                for b in range(bb)]

    def softmax(s):
        p = jnp.exp(s - jnp.max(s, axis=-1, keepdims=True))
        return (p * (1.0 / jnp.sum(p, axis=-1, keepdims=True))).astype(BF16)

    def attend(s):
        if cache_layout:
            outs = []
            for b in range(bb):
                o4 = _dot(softmax(s[b][0]), vh[b])
                outs.append(jnp.concatenate([o4[hh * rows:(hh + 1) * rows] for hh in range(XA_HEADS)], axis=1))
        else:
            outs = [jnp.concatenate([_dot(softmax(s[b][hh]), vh[b][hh]) for hh in range(XA_HEADS)], axis=1)
                    for b in range(bb)]
        return (jnp.concatenate(outs, axis=0) if bb > 1 else outs[0]).astype(BF16)

    x, q = load(0)
    s = scores(q)
    for r in range(parts):
        if r + 1 < parts:
            x_next, q = load(r + 1)
        o = attend(s)
        if r + 1 < parts:
            s = scores(q)
        a = _dot(o, wo_ref[...])
        o_ref[:, r * rows:(r + 1) * rows, :] = (x + _rms(a, gpost_ref[...])).reshape(bb, rows, d)
        if r + 1 < parts:
            x = x_next


def _xattn(x, mem_k, mem_v, norm_pre, norm_post, wq, wo, i, bb, tl):
    b, l, d = x.shape
    x_spec = pl.BlockSpec((bb, tl, d), lambda bi, ti: (bi, ti, 0))
    kv_tail = mem_k.shape[2:]
    kv_spec = pl.BlockSpec((None, bb) + kv_tail, lambda bi, ti: (i, bi) + (0,) * len(kv_tail))
    return pl.pallas_call(
        functools.partial(_xattn_kernel, bb=bb, tl=tl),
        grid=(b // bb, l // tl),
        in_specs=[x_spec, kv_spec, kv_spec, _fixed((1, d), i, 2), _fixed((1, d), i, 2),
                  _fixed((d, d), i), _fixed((d, d), i)],
        out_specs=x_spec,
        out_shape=jax.ShapeDtypeStruct((b, l, d), F32),
        compiler_params=_params("parallel", "parallel"),
        name=f"xattn_l{i}",
    )(x, mem_k, mem_v, norm_pre, norm_post, wq, wo)


def _conv_kernel(x_ref, cbuf_ref, gpre_ref, gpost_ref, win_ref, bin_ref, dw_ref, dwb_ref, lng_ref, lnb_ref,
                 wout_ref, bout_ref, o_ref, cout_ref, hist_scr, c_scr, *, bb, tl):
    d = x_ref.shape[-1]
    m = bb * tl
    hist0 = CA_HIST_OFF - (CA_CONV_W - 1)
    t = pl.program_id(1)

    @pl.when(t == 0)
    def _():
        hist_scr[:, 0, 0:SUBLANES, :] = jnp.zeros((bb, SUBLANES, d), F32)
        hist_scr[:, 0, hist0:CA_HIST_OFF, :] = cbuf_ref[...]

    x = x_ref[...].reshape(m, d)
    h = _rms(x, gpre_ref[...]).astype(BF16)
    u = _dot(h, win_ref[...]) + bin_ref[...]
    glu = u[:, :d] * jax.nn.sigmoid(u[:, d:])
    hist_scr[:, 0, CA_HIST_OFF:CA_HIST_OFF + tl, :] = glu.reshape(bb, tl, d)
    n_shift = tl + CA_HIST_OFF - SUBLANES
    for s in range(1, SUBLANES):
        hist_scr[:, s, 0:n_shift, :] = hist_scr[:, 0, s:s + n_shift, :]

    rb = min(CA_ROW_BLOCK, tl)
    groups = rb // SUBLANES
    for b in range(bb):
        for lc in range(d // LANES):
            ls = slice(lc * LANES, (lc + 1) * LANES)
            taps = [jnp.broadcast_to(dw_ref[j:j + 1, ls], (SUBLANES, LANES)) for j in range(CA_CONV_W)]
            bias = jnp.broadcast_to(dwb_ref[:, ls], (SUBLANES, LANES))

            def row_block(r, carry, b=b, ls=ls, taps=taps, bias=bias):
                r0 = pl.multiple_of(r * rb, rb)
                for g in range(groups):
                    acc = bias
                    for j in range(CA_CONV_W):
                        s, a = (hist0 + j) % SUBLANES, (hist0 + j) // SUBLANES
                        acc = acc + taps[j] * hist_scr[b, s, pl.ds(r0 + (a + g) * SUBLANES, SUBLANES), ls]
                    c_scr[pl.ds(b * tl + r0 + g * SUBLANES, SUBLANES), ls] = acc
                return carry
            lax.fori_loop(0, tl // rb, row_block, 0)

    c = c_scr[...]
    mu = jnp.mean(c, axis=-1, keepdims=True)
    cc = c - mu
    ln = cc * lax.rsqrt(jnp.mean(cc * cc, axis=-1, keepdims=True) + EPS) * lng_ref[...] + lnb_ref[...]
    y = _dot(_silu(ln).astype(BF16), wout_ref[...]) + bout_ref[...]
    o_ref[...] = (x + _rms(y, gpost_ref[...])).reshape(bb, tl, d)

    tail = hist_scr[:, 0, tl + hist0:tl + CA_HIST_OFF, :]
    hist_scr[:, 0, hist0:CA_HIST_OFF, :] = tail

    @pl.when(t == pl.num_programs(1) - 1)
    def _():
        cout_ref[...] = tail


def _conv(x, conv_buf, norm_pre, norm_post, win, b_in, dw, dw_b, ln_g, ln_b, wout, b_out, i, j, bb, tl):
    b, l, d = x.shape
    nh = CA_CONV_W - 1
    x_spec = pl.BlockSpec((bb, tl, d), lambda bi, ti: (bi, ti, 0))
    c_spec = pl.BlockSpec((None, bb, nh, d), lambda bi, ti: (j, bi, 0, 0))
    return pl.pallas_call(
        functools.partial(_conv_kernel, bb=bb, tl=tl),
        grid=(b // bb, l // tl),
        in_specs=[x_spec, c_spec, _fixed((1, d), i, 1), _fixed((1, d), i, 1),
                  _fixed((d, 2 * d), j), _fixed((1, 2 * d), j), _fixed((CA_CONV_W, d), j), _fixed((1, d), j),
                  _fixed((1, d), j), _fixed((1, d), j), _fixed((d, d), j), _fixed((1, d), j)],
        out_specs=[x_spec, pl.BlockSpec((bb, nh, d), lambda bi, ti: (bi, 0, 0))],
        out_shape=[jax.ShapeDtypeStruct((b, l, d), F32), jax.ShapeDtypeStruct((b, nh, d), F32)],
        scratch_shapes=[pltpu.VMEM((bb, SUBLANES, CA_HIST_OFF + tl, d), F32), pltpu.VMEM((bb * tl, d), F32)],
        compiler_params=_params("parallel", "arbitrary"),
        name=f"conv_l{i}",
    )(x, conv_buf, norm_pre, norm_post, win, b_in, dw, dw_b, ln_g, ln_b, wout, b_out)


def _unit_lower_inverses(ns, c):
    ri = lax.broadcasted_iota(jnp.int32, (c, c), 0)
    ci = lax.broadcasted_iota(jnp.int32, (c, c), 1)
    shift = DN_INV_BLOCK.bit_length() - 1
    eye = (ri == ci).astype(F32)
    same = (ri >> shift) == (ci >> shift)
    mm = lambda a, b: _dot(a.astype(BF16), b.astype(BF16))
    xs = [jnp.where(same, n, 0.0) for n in ns]
    ps = [eye - x for x in xs]
    size = 2
    while size < DN_INV_BLOCK:
        xs = [mm(x, x) for x in xs]
        ps = [p + px for p, px in zip(ps, [mm(p, x) for p, x in zip(ps, xs)])]
        size *= 2
    size = DN_INV_BLOCK
    while size < c:
        s = size.bit_length() - 1
        rblk = ri >> s
        off = ((rblk & 1) == 1) & ((ci >> s) == rblk - 1)
        aps = [mm(jnp.where(off, n, 0.0), p) for n, p in zip(ns, ps)]
        ps = [p - pap for p, pap in zip(ps, [mm(p, ap) for p, ap in zip(ps, aps)])]
        size *= 2
    return ps


def _dn_kernel(x_ref, s0_ref, cbuf_ref, gpre_ref, gpost_ref, wqkvg_ref, wab_ref, cw_ref, alog_ref, dtb_ref,
               ng_ref, wout_ref, o_ref, sout_ref, cout_ref, hist_scr, s_scr, *, bb, tl):
    d = x_ref.shape[-1]
    hd = DN_HEAD_DIM
    w = DN_WIDTH
    m = bb * tl
    c = DN_CHUNK
    nc = -(-tl // c)
    grp = min(c, tl)
    nck = bb * nc
    hist0 = DN_HIST_OFF - (DN_CONV_W - 1)
    t = pl.program_id(1)

    @pl.when(t == 0)
    def _():
        s_scr[...] = s0_ref[...]
        hist_scr[:, hist0:DN_HIST_OFF, :] = cbuf_ref[...]

    x = x_ref[...].reshape(m, d)
    h = _rms(x, gpre_ref[...]).astype(BF16)
    proj = _dot(h, wqkvg_ref[...])
    ab = _dot(h, wab_ref[...])
    hist_scr[:, DN_HIST_OFF:DN_HIST_OFF + tl, :] = proj[:, :3 * w].reshape(bb, tl, 3 * w)
    conv = cw_ref[0:1, :] * hist_scr[:, hist0:hist0 + tl, :]
    for j in range(1, DN_CONV_W):
        conv = conv + cw_ref[j:j + 1, :] * hist_scr[:, hist0 + j:hist0 + j + tl, :]
    qkv = _silu(conv).reshape(m, 3 * w)
    tail = hist_scr[:, tl + hist0:tl + DN_HIST_OFF, :]
    hist_scr[:, hist0:DN_HIST_OFF, :] = tail

    abt = ab.T
    g_t = -jnp.exp(alog_ref[...]) * jax.nn.softplus(abt + dtb_ref[...])
    pos = lax.broadcasted_iota(jnp.int32, g_t.shape, 1) & (grp - 1)
    gc_t = g_t
    step = 1
    while step < grp:
        gc_t = gc_t + jnp.where(pos >= step, pltpu.roll(gc_t, step, axis=1), 0.0)
        step *= 2
    gc_col = gc_t.T
    beta_col = jax.nn.sigmoid(ab)

    ri = lax.broadcasted_iota(jnp.int32, (c, c), 0)
    ci_ = lax.broadcasted_iota(jnp.int32, (c, c), 1)
    causal = ri >= ci_
    strict = ri > ci_
    last = grp - 1
    bf = lambda a: a.astype(BF16)

    def chunk_rows(a, ck):
        blk = a[ck * grp:(ck + 1) * grp]
        if grp == c:
            return blk
        return jnp.concatenate([blk, jnp.zeros((c - grp, a.shape[1]), a.dtype)], axis=0)

    qn, kn = [], []
    for hh in range(DN_HEADS):
        qh = qkv[:, hh * hd:(hh + 1) * hd]
        kh = qkv[:, w + hh * hd:w + (hh + 1) * hd]
        qn.append(qh * lax.rsqrt(jnp.sum(qh * qh, axis=-1, keepdims=True) + EPS) * (hd ** -0.5))
        kn.append(kh * lax.rsqrt(jnp.sum(kh * kh, axis=-1, keepdims=True) + EPS))

    probs = [(ck, hh) for ck in range(nck) for hh in range(DN_HEADS)]
    gc_c, eg_c, egl_c, ge_c, beta_c, gct_c = [], [], [], [], [], []
    for ck in range(nck):
        g = chunk_rows(gc_col, ck)
        g_last = g[last:last + 1, :]
        gc_c.append(g)
        eg_c.append(jnp.exp(g))
        egl_c.append(jnp.exp(g_last - g))
        ge_c.append(jnp.exp(g_last))
        beta_c.append(chunk_rows(beta_col, ck))
        row = gc_t[0:DN_HEADS, ck * grp:(ck + 1) * grp]
        gct_c.append(row if grp == c else jnp.concatenate([row, jnp.zeros((DN_HEADS, c - grp), F32)], axis=1))
    q_p = [chunk_rows(qn[hh], ck) for ck, hh in probs]
    k_p = [chunk_rows(kn[hh], ck) for ck, hh in probs]
    v_p = [chunk_rows(qkv[:, 2 * w + hh * hd:2 * w + (hh + 1) * hd], ck) for ck, hh in probs]
    beta_p = [beta_c[ck][:, DN_HEADS + hh:DN_HEADS + hh + 1] for ck, hh in probs]
    eg_p = [eg_c[ck][:, hh:hh + 1] for ck, hh in probs]
    dec_p = [jnp.where(causal, jnp.exp(jnp.minimum(gc_c[ck][:, hh:hh + 1] - gct_c[ck][hh:hh + 1, :], 0.0)), 0.0)
             for ck, hh in probs]
    kb_p = [k * beta for k, beta in zip(k_p, beta_p)]
    kkqk_p = [_dot_nt(bf(jnp.concatenate([kb, q], axis=0)), bf(k)) for kb, q, k in zip(kb_p, q_p, k_p)]
    a_p = [jnp.where(strict, kkqk[:c] * dec, 0.0) for kkqk, dec in zip(kkqk_p, dec_p)]
    qk_p = [bf(kkqk[c:] * dec) for kkqk, dec in zip(kkqk_p, dec_p)]
    tinv_p = _unit_lower_inverses(a_p, c)
    rhs_p = [bf(jnp.concatenate([v * beta, kb * eg], axis=1)) for v, beta, kb, eg in zip(v_p, beta_p, kb_p, eg_p)]
    sol_p = [_dot(bf(t_inv), rhs) for t_inv, rhs in zip(tinv_p, rhs_p)]
    kcq_p = [bf(jnp.concatenate([sol[:, hd:], q * eg], axis=0)) for sol, q, eg in zip(sol_p, q_p, eg_p)]
    kd_p = [bf(k * egl_c[ck][:, hh:hh + 1]) for k, (ck, hh) in zip(k_p, probs)]

    o_p = [None] * len(probs)
    for ci in range(nc):
        sel = [(b, hh, (b * nc + ci) * DN_HEADS + hh) for b in range(bb) for hh in range(DN_HEADS)]
        s_old = [s_scr[b, hh] for b, hh, _ in sel]
        r = [_dot(kcq_p[p], bf(s)) for (_, _, p), s in zip(sel, s_old)]
        ub = [bf(sol_p[p][:, :hd] - rr[:c]) for (_, _, p), rr in zip(sel, r)]
        for (b, hh, p), rr, u, s in zip(sel, r, ub, s_old):
            o_p[p] = rr[c:] + _dot(qk_p[p], u)
            s_scr[b, hh] = s * ge_c[p // DN_HEADS][:, hh:hh + 1] + _dot_tn(kd_p[p], u)

    o = jnp.concatenate(
        [jnp.concatenate(o_p[ck * DN_HEADS:(ck + 1) * DN_HEADS], axis=1)[0:grp] for ck in range(nck)], axis=0)
    gate = proj[:, 3 * w:]
    ng = ng_ref[...]
    outs = []
    for hh in range(DN_HEADS):
        sl = slice(hh * hd, (hh + 1) * hd)
        outs.append(_rms(o[:, sl], ng) * _silu(gate[:, sl]))
    y = _dot(jnp.concatenate(outs, axis=1).astype(BF16), wout_ref[...])
    o_ref[...] = (x + _rms(y, gpost_ref[...])).reshape(bb, tl, d)

    @pl.when(t == pl.num_programs(1) - 1)
    def _():
        sout_ref[...] = s_scr[...]
        cout_ref[...] = tail


def _dn(x, s0, conv_buf, norm_pre, norm_post, wqkvg, wab, cw, alog, dtb, ng, wout, i, j, bb, tl):
    b, l, d = x.shape
    hd, nh, w = DN_HEAD_DIM, DN_HEADS, DN_WIDTH
    nhist = DN_CONV_W - 1
    x_spec = pl.BlockSpec((bb, tl, d), lambda bi, ti: (bi, ti, 0))
    s_in_spec = pl.BlockSpec((None, bb, nh, hd, hd), lambda bi, ti: (j, bi, 0, 0, 0))
    c_in_spec = pl.BlockSpec((None, bb, nhist, 3 * w), lambda bi, ti: (j, bi, 0, 0))
    return pl.pallas_call(
        functools.partial(_dn_kernel, bb=bb, tl=tl),
        grid=(b // bb, l // tl),
        in_specs=[x_spec, s_in_spec, c_in_spec, _fixed((1, d), i, 1), _fixed((1, d), i, 1),
                  _fixed((d, 4 * w), j), _fixed((d, LANES), j), _fixed((DN_CONV_W, 3 * w), j),
                  _fixed((LANES, 1), j), _fixed((LANES, 1), j), _fixed((1, hd), j), _fixed((w, d), j)],
        out_specs=[x_spec,
                   pl.BlockSpec((bb, nh, hd, hd), lambda bi, ti: (bi, 0, 0, 0)),
                   pl.BlockSpec((bb, nhist, 3 * w), lambda bi, ti: (bi, 0, 0))],
        out_shape=[jax.ShapeDtypeStruct((b, l, d), F32),
                   jax.ShapeDtypeStruct((b, nh, hd, hd), F32),
                   jax.ShapeDtypeStruct((b, nhist, 3 * w), F32)],
        scratch_shapes=[pltpu.VMEM((bb, DN_HIST_OFF + tl, 3 * w), F32), pltpu.VMEM((bb, nh, hd, hd), F32)],
        compiler_params=_params("parallel", "arbitrary"),
        name=f"deltanet_l{i}",
    )(x, s0, conv_buf, norm_pre, norm_post, wqkvg, wab, cw, alog, dtb, ng, wout)


def _tiles(b, l):
    if l >= 512:
        return 1024, 1, 512, min(l, 2048), 2, 128
    bb = max(1, min(b, LANES // l))
    return min(b * l, 512), bb, l, l, bb, l


def _trunk(x, mem_k, mem_v, conv_bufs, dn_states, dn_bufs, P):
    b, l, d = x.shape
    depth = P["wq"].shape[0]
    tm, bb, tl, tl_xa, bb_dn, tl_dn = _tiles(b, l)
    ffn = lambda x, i, k, j: _ffn(x.reshape(b * l, d), P["norm_pre"], P["norm_post"], P["wg"], P["wu"], P["wd"],
                                  i, k, j, tm).reshape(b, l, d)
    new_conv, new_states, new_dn_bufs = [], [], []
    conv_args = (P["ca_win"], P["ca_b_in"], P["ca_dw"], P["ca_dw_b"], P["ca_ln_g"], P["ca_ln_b"], P["ca_wout"],
                 P["ca_b_out"])
    for i in range(depth):
        j = i // 2
        x = ffn(x, i, 0, 0)
        if i % 2 == 0:
            x, buf = _conv(x, conv_bufs, P["norm_pre"], P["norm_post"], *conv_args, i, j, bb, tl)
            new_conv.append(buf)
        else:
            x, s, buf = _dn(x, dn_states, dn_bufs, P["norm_pre"], P["norm_post"], P["dn_wqkvg"], P["dn_wab"],
                            P["dn_conv_w"], P["dn_alog"], P["dn_dtb"], P["dn_norm_g"], P["dn_wout"], i, j,
                            bb_dn, tl_dn)
            new_states.append(s)
            new_dn_bufs.append(buf)
        x = _xattn(x, mem_k, mem_v, P["norm_pre"], P["norm_post"], P["wq"], P["wo"], i, bb, tl_xa)
        x = ffn(x, i, 3, 1)
    return x, jnp.stack(new_conv), jnp.stack(new_states), jnp.stack(new_dn_bufs)


def _prep(norm_pre, norm_post, ffn_w_gate, ffn_w_up, ffn_w_down, xa_wq, xa_wo,
          ca_w_in, ca_b_in, ca_dw, ca_dw_b, ca_ln_g, ca_ln_b, ca_w_out, ca_b_out,
          dn_w_in, dn_conv_w, dn_a_log, dn_dt_bias, dn_norm_g, dn_w_out):
    w, nh = DN_WIDTH, DN_HEADS
    row = lambda p: p[..., None, :]
    col = lambda p: jnp.pad(p, ((0, 0), (0, LANES - nh)))[..., None]
    return dict(
        norm_pre=row(norm_pre), norm_post=row(norm_post),
        wg=ffn_w_gate.astype(BF16), wu=ffn_w_up.astype(BF16), wd=ffn_w_down.astype(BF16),
        wq=xa_wq.astype(BF16), wo=xa_wo.astype(BF16),
        ca_win=ca_w_in.astype(BF16), ca_b_in=row(ca_b_in), ca_dw=ca_dw, ca_dw_b=row(ca_dw_b),
        ca_ln_g=row(ca_ln_g), ca_ln_b=row(ca_ln_b), ca_wout=ca_w_out.astype(BF16), ca_b_out=row(ca_b_out),
        dn_wqkvg=dn_w_in[:, :, :4 * w].astype(BF16),
        dn_wab=jnp.pad(dn_w_in[:, :, 4 * w:], ((0, 0), (0, 0), (0, LANES - 2 * nh))).astype(BF16),
        dn_conv_w=dn_conv_w, dn_alog=col(dn_a_log), dn_dtb=col(dn_dt_bias), dn_norm_g=row(dn_norm_g),
        dn_wout=dn_w_out.astype(BF16),
    )


def kernel(x_prompt, x_sample, mem_prompt, cache_conv_a, state_dn, cache_dn_conv, cache_mem_k, cache_mem_v,
           norm_pre, norm_post, ffn_w_gate, ffn_w_up, ffn_w_down, xa_wq, xa_wk, xa_wv, xa_wo,
           ca_w_in, ca_b_in, ca_dw, ca_dw_b, ca_ln_g, ca_ln_b, ca_w_out, ca_b_out,
           dn_w_in, dn_conv_w, dn_a_log, dn_dt_bias, dn_norm_g, dn_w_out):
    bp = x_prompt.shape[0]
    n_conv, n_dn = ca_w_in.shape[0], dn_w_in.shape[0]
    P = _prep(norm_pre, norm_post, ffn_w_gate, ffn_w_up, ffn_w_down, xa_wq, xa_wo,
              ca_w_in, ca_b_in, ca_dw, ca_dw_b, ca_ln_g, ca_ln_b, ca_w_out, ca_b_out,
              dn_w_in, dn_conv_w, dn_a_log, dn_dt_bias, dn_norm_g, dn_w_out)
    p_k, p_v, p_k16, p_v16 = _memkv(mem_prompt, xa_wk.astype(BF16), xa_wv.astype(BF16), MEMKV_BATCH_ROWS)
    zero_conv = jnp.zeros((n_conv, bp) + cache_conv_a.shape[2:], F32)
    zero_state = jnp.zeros((n_dn, bp) + state_dn.shape[2:], F32)
    zero_dn_conv = jnp.zeros((n_dn, bp) + cache_dn_conv.shape[2:], F32)
    y_p, p_conv_a, p_state_dn, p_dn_conv = _trunk(x_prompt, p_k16, p_v16, zero_conv, zero_state, zero_dn_conv, P)
    y_s, s_conv_a, s_state_dn, s_dn_conv = _trunk(
        x_sample, cache_mem_k, cache_mem_v, cache_conv_a, state_dn, cache_dn_conv, P)
    return (y_p, y_s, p_conv_a, p_state_dn, p_dn_conv, p_k, p_v, s_conv_a, s_state_dn, s_dn_conv)
```

```python
import functools

import jax
import jax.numpy as jnp
from jax import lax
from jax.experimental import pallas as pl
from jax.experimental.pallas import tpu as pltpu

F32 = jnp.float32
BF16 = jnp.bfloat16
EPS = 1e-6

DN_HEADS = 8
DN_HEAD_DIM = 128
DN_WIDTH = DN_HEADS * DN_HEAD_DIM
DN_CHUNK = 64
DN_INV_BLOCK = 16
DN_CONV_W = 4
DN_HIST_OFF = 8
CA_CONV_W = 31
CA_HIST_OFF = 32
CA_ROW_BLOCK = 128
XA_HEADS = 4
XA_PART_ROWS = 256
FFN_PART_ROWS = 256
MEMKV_BATCH_ROWS = 2
LANES = 128
SUBLANES = 8

FFN_TILE_ROWS = 1024
CA_TILE_ROWS = 512
XA_TILE_ROWS = 2048
DN_TILE_SEQS, DN_TILE_ROWS = 2, 128
LONG_SEQ = 512

VMEM_LIMIT_BYTES = 56 * 1024 * 1024


def _rms(x, g):
    return x * lax.rsqrt(jnp.mean(x * x, axis=-1, keepdims=True) + EPS) * g


def _silu(x):
    return x * jax.nn.sigmoid(x)


def _dot(a, b):
    return jnp.dot(a, b, preferred_element_type=F32)


def _dot_nt(a, b):
    return lax.dot_general(a, b, (((1,), (1,)), ((), ())), preferred_element_type=F32)


def _dot_tn(a, b):
    return lax.dot_general(a, b, (((0,), (0,)), ((), ())), preferred_element_type=F32)


def _fixed(tail, *lead):
    n_tail = len(tail)
    idx = tuple(lead) + (0,) * n_tail
    return pl.BlockSpec((None,) * len(lead) + tuple(tail), lambda *_: idx, pipeline_mode=pl.Buffered(1))


def _params(*sem):
    return pltpu.CompilerParams(dimension_semantics=sem, vmem_limit_bytes=VMEM_LIMIT_BYTES)


def _ffn_kernel(x_ref, gpre_ref, gpost_ref, wg_ref, wu_ref, wd_ref, o_ref):
    tm = x_ref.shape[0]
    rows = min(FFN_PART_ROWS, tm)
    for r in range(tm // rows):
        rs = slice(r * rows, (r + 1) * rows)
        x = x_ref[rs, :]
        h = _rms(x, gpre_ref[...]).astype(BF16)
        g = _dot(h, wg_ref[...])
        u = _dot(h, wu_ref[...])
        a = (_silu(g) * u).astype(BF16)
        f = _dot(a, wd_ref[...])
        o_ref[rs, :] = x + 0.5 * _rms(f, gpost_ref[...])


def _ffn(x2d, norm_pre, norm_post, wg, wu, wd, i, k, j, tm):
    n, d = x2d.shape
    dff = wg.shape[-1]
    return pl.pallas_call(
        _ffn_kernel,
        grid=(n // tm,),
        in_specs=[
            pl.BlockSpec((tm, d), lambda m: (m, 0)),
            _fixed((1, d), i, k),
            _fixed((1, d), i, k),
            _fixed((d, dff), i, j),
            _fixed((d, dff), i, j),
            _fixed((dff, d), i, j),
        ],
        out_specs=pl.BlockSpec((tm, d), lambda m: (m, 0)),
        out_shape=jax.ShapeDtypeStruct((n, d), F32),
        compiler_params=_params("parallel"),
        name=f"ffn_l{i}_{j}",
    )(x2d, norm_pre, norm_post, wg, wu, wd)


def _memkv_kernel(m_ref, wk_ref, wv_ref, k_ref, v_ref, kb_ref, vb_ref):
    tb, nm, nh, dh = k_ref.shape
    m = m_ref[...].astype(BF16)
    for w_ref, o_ref, ob_ref in ((wk_ref, k_ref, kb_ref), (wv_ref, v_ref, vb_ref)):
        y3 = _dot(m, w_ref[...]).reshape(tb, nm, nh * dh)
        ob_ref[...] = y3.astype(BF16)
        for hh in range(nh):
            o_ref[:, :, hh, :] = y3[:, :, hh * dh:(hh + 1) * dh]


def _memkv(mem, wk, wv, tb):
    b, nm, d = mem.shape
    depth = wk.shape[0]
    nh, dh = XA_HEADS, d // XA_HEADS
    w_spec = pl.BlockSpec((None, d, d), lambda l, m: (l, 0, 0))
    o_spec = pl.BlockSpec((None, tb, nm, nh, dh), lambda l, m: (l, m, 0, 0, 0))
    ob_spec = pl.BlockSpec((None, tb, nm, d), lambda l, m: (l, m, 0, 0))
    return pl.pallas_call(
        _memkv_kernel,
        grid=(depth, b // tb),
        in_specs=[pl.BlockSpec((tb * nm, d), lambda l, m: (m, 0)), w_spec, w_spec],
        out_specs=[o_spec, o_spec, ob_spec, ob_spec],
        out_shape=[jax.ShapeDtypeStruct((depth, b, nm, nh, dh), F32)] * 2
        + [jax.ShapeDtypeStruct((depth, b, nm, d), BF16)] * 2,
        compiler_params=_params("parallel", "parallel"),
        name="mem_kv",
    )(mem.reshape(b * nm, d), wk, wv)


def _xattn_kernel(x_ref, k_ref, v_ref, gpre_ref, gpost_ref, wq_ref, wo_ref, o_ref, *, bb, tl):
    d = x_ref.shape[-1]
    dh = d // XA_HEADS
    rows = min(XA_PART_ROWS, tl)
    parts = tl // rows
    cache_layout = len(k_ref.shape) == 4
    nm = k_ref.shape[1]
    hs = [slice(hh * dh, (hh + 1) * dh) for hh in range(XA_HEADS)]

    if cache_layout:
        kh = [k_ref[b].reshape(nm * XA_HEADS, dh).astype(BF16) for b in range(bb)]
        vh = [v_ref[b].reshape(nm * XA_HEADS, dh).astype(BF16) for b in range(bb)]
        shape = (XA_HEADS * rows, nm * XA_HEADS)
        same_head = (lax.broadcasted_iota(jnp.int32, shape, 1) & (XA_HEADS - 1)) == (
            lax.broadcasted_iota(jnp.int32, shape, 0) // rows)
    else:
        kh = [[k_ref[b, :, sl] for sl in hs] for b in range(bb)]
        vh = [[v_ref[b, :, sl] for sl in hs] for b in range(bb)]

    def load(r):
        x = x_ref[:, r * rows:(r + 1) * rows, :].reshape(bb * rows, d)
        return x, _dot(_rms(x, gpre_ref[...]).astype(BF16), wq_ref[...]).astype(BF16)

    def scores(q):
        if cache_layout:
            out = []
            for b in range(bb):
                qb = q[b * rows:(b + 1) * rows]
                s = _dot_nt(jnp.concatenate([qb[:, sl] for sl in hs], axis=0), kh[b]) * (dh ** -0.5)
                out.append([jnp.where(same_head, s, -jnp.inf)])
            return out
        return [[_dot_nt(q[b * rows:(b + 1) * rows, sl], kh[b][hh]) * (dh ** -0.5) for hh, sl in enumerate(hs)]
                for b in range(bb)]

    def softmax(s):
        p = jnp.exp(s - jnp.max(s, axis=-1, keepdims=True))
        return (p * (1.0 / jnp.sum(p, axis=-1, keepdims=True))).astype(BF16)

    def attend(s):
        if cache_layout:
            outs = []
            for b in range(bb):
                o4 = _dot(softmax(s[b][0]), vh[b])
                outs.append(jnp.concatenate([o4[hh * rows:(hh + 1) * rows] for hh in range(XA_HEADS)], axis=1))
        else:
            outs = [jnp.concatenate([_dot(softmax(s[b][hh]), vh[b][hh]) for hh in range(XA_HEADS)], axis=1)
                    for b in range(bb)]
        return (jnp.concatenate(outs, axis=0) if bb > 1 else outs[0]).astype(BF16)

    x, q = load(0)
    s = scores(q)
    for r in range(parts):
        if r + 1 < parts:
            x_next, q = load(r + 1)
        o = attend(s)
        if r + 1 < parts:
            s = scores(q)
        a = _dot(o, wo_ref[...])
        o_ref[:, r * rows:(r + 1) * rows, :] = (x + _rms(a, gpost_ref[...])).reshape(bb, rows, d)
        if r + 1 < parts:
            x = x_next


def _xattn(x, mem_k, mem_v, norm_pre, norm_post, wq, wo, i, bb, tl):
    b, l, d = x.shape
    x_spec = pl.BlockSpec((bb, tl, d), lambda bi, ti: (bi, ti, 0))
    kv_tail = mem_k.shape[2:]
    kv_spec = pl.BlockSpec((None, bb) + kv_tail, lambda bi, ti: (i, bi) + (0,) * len(kv_tail))
    return pl.pallas_call(
        functools.partial(_xattn_kernel, bb=bb, tl=tl),
        grid=(b // bb, l // tl),
        in_specs=[x_spec, kv_spec, kv_spec, _fixed((1, d), i, 2), _fixed((1, d), i, 2),
                  _fixed((d, d), i), _fixed((d, d), i)],
        out_specs=x_spec,
        out_shape=jax.ShapeDtypeStruct((b, l, d), F32),
        compiler_params=_params("parallel", "parallel"),
        name=f"xattn_l{i}",
    )(x, mem_k, mem_v, norm_pre, norm_post, wq, wo)


def _conv_kernel(x_ref, cbuf_ref, gpre_ref, gpost_ref, win_ref, bin_ref, dw_ref, dwb_ref, lng_ref, lnb_ref,
                 wout_ref, bout_ref, o_ref, cout_ref, hist_scr, c_scr, *, bb, tl):
    d = x_ref.shape[-1]
    m = bb * tl
    hist0 = CA_HIST_OFF - (CA_CONV_W - 1)
    t = pl.program_id(1)

    @pl.when(t == 0)
    def _():
        hist_scr[:, 0, 0:SUBLANES, :] = jnp.zeros((bb, SUBLANES, d), F32)
        hist_scr[:, 0, hist0:CA_HIST_OFF, :] = cbuf_ref[...]

    x = x_ref[...].reshape(m, d)
    h = _rms(x, gpre_ref[...]).astype(BF16)
    u = _dot(h, win_ref[...]) + bin_ref[...]
    glu = u[:, :d] * jax.nn.sigmoid(u[:, d:])
    hist_scr[:, 0, CA_HIST_OFF:CA_HIST_OFF + tl, :] = glu.reshape(bb, tl, d)
    n_shift = tl + CA_HIST_OFF - SUBLANES
    for s in range(1, SUBLANES):
        hist_scr[:, s, 0:n_shift, :] = hist_scr[:, 0, s:s + n_shift, :]

    rb = min(CA_ROW_BLOCK, tl)
    groups = rb // SUBLANES
    for b in range(bb):
        for lc in range(d // LANES):
            ls = slice(lc * LANES, (lc + 1) * LANES)
            taps = [jnp.broadcast_to(dw_ref[j:j + 1, ls], (SUBLANES, LANES)) for j in range(CA_CONV_W)]
            bias = jnp.broadcast_to(dwb_ref[:, ls], (SUBLANES, LANES))

            def row_block(r, carry, b=b, ls=ls, taps=taps, bias=bias):
                r0 = pl.multiple_of(r * rb, rb)
                for g in range(groups):
                    acc = bias
                    for j in range(CA_CONV_W):
                        s, a = (hist0 + j) % SUBLANES, (hist0 + j) // SUBLANES
                        acc = acc + taps[j] * hist_scr[b, s, pl.ds(r0 + (a + g) * SUBLANES, SUBLANES), ls]
                    c_scr[pl.ds(b * tl + r0 + g * SUBLANES, SUBLANES), ls] = acc
                return carry
            lax.fori_loop(0, tl // rb, row_block, 0)

    c = c_scr[...]
    mu = jnp.mean(c, axis=-1, keepdims=True)
    cc = c - mu
    ln = cc * lax.rsqrt(jnp.mean(cc * cc, axis=-1, keepdims=True) + EPS) * lng_ref[...] + lnb_ref[...]
    y = _dot(_silu(ln).astype(BF16), wout_ref[...]) + bout_ref[...]
    o_ref[...] = (x + _rms(y, gpost_ref[...])).reshape(bb, tl, d)

    tail = hist_scr[:, 0, tl + hist0:tl + CA_HIST_OFF, :]
    hist_scr[:, 0, hist0:CA_HIST_OFF, :] = tail

    @pl.when(t == pl.num_programs(1) - 1)
    def _():
        cout_ref[...] = tail


def _conv(x, conv_buf, norm_pre, norm_post, win, b_in, dw, dw_b, ln_g, ln_b, wout, b_out, i, j, bb, tl):
    b, l, d = x.shape
    nh = CA_CONV_W - 1
    x_spec = pl.BlockSpec((bb, tl, d), lambda bi, ti: (bi, ti, 0))
    c_spec = pl.BlockSpec((None, bb, nh, d), lambda bi, ti: (j, bi, 0, 0))
    return pl.pallas_call(
        functools.partial(_conv_kernel, bb=bb, tl=tl),
        grid=(b // bb, l // tl),
        in_specs=[x_spec, c_spec, _fixed((1, d), i, 1), _fixed((1, d), i, 1),
                  _fixed((d, 2 * d), j), _fixed((1, 2 * d), j), _fixed((CA_CONV_W, d), j), _fixed((1, d), j),
                  _fixed((1, d), j), _fixed((1, d), j), _fixed((d, d), j), _fixed((1, d), j)],
        out_specs=[x_spec, pl.BlockSpec((bb, nh, d), lambda bi, ti: (bi, 0, 0))],
        out_shape=[jax.ShapeDtypeStruct((b, l, d), F32), jax.ShapeDtypeStruct((b, nh, d), F32)],
        scratch_shapes=[pltpu.VMEM((bb, SUBLANES, CA_HIST_OFF + tl, d), F32), pltpu.VMEM((bb * tl, d), F32)],
        compiler_params=_params("parallel", "arbitrary"),
        name=f"conv_l{i}",
    )(x, conv_buf, norm_pre, norm_post, win, b_in, dw, dw_b, ln_g, ln_b, wout, b_out)


def _unit_lower_inverses(ns, c):
    ri = lax.broadcasted_iota(jnp.int32, (c, c), 0)
    ci = lax.broadcasted_iota(jnp.int32, (c, c), 1)
    shift = DN_INV_BLOCK.bit_length() - 1
    eye = (ri == ci).astype(F32)
    same = (ri >> shift) == (ci >> shift)
    mm = lambda a, b: _dot(a.astype(BF16), b.astype(BF16))
    xs = [jnp.where(same, n, 0.0) for n in ns]
    ps = [eye - x for x in xs]
    size = 2
    while size < DN_INV_BLOCK:
        xs = [mm(x, x) for x in xs]
        ps = [p + px for p, px in zip(ps, [mm(p, x) for p, x in zip(ps, xs)])]
        size *= 2
    size = DN_INV_BLOCK
    while size < c:
        s = size.bit_length() - 1
        rblk = ri >> s
        off = ((rblk & 1) == 1) & ((ci >> s) == rblk - 1)
        aps = [mm(jnp.where(off, n, 0.0), p) for n, p in zip(ns, ps)]
        ps = [p - pap for p, pap in zip(ps, [mm(p, ap) for p, ap in zip(ps, aps)])]
        size *= 2
    return ps


def _dn_kernel(x_ref, s0_ref, cbuf_ref, gpre_ref, gpost_ref, wqkvg_ref, wab_ref, cw_ref, alog_ref, dtb_ref,
               ng_ref, wout_ref, o_ref, sout_ref, cout_ref, hist_scr, s_scr, *, bb, tl):
    d = x_ref.shape[-1]
    hd = DN_HEAD_DIM
    w = DN_WIDTH
    m = bb * tl
    c = DN_CHUNK
    nc = -(-tl // c)
    grp = min(c, tl)
    nck = bb * nc
    hist0 = DN_HIST_OFF - (DN_CONV_W - 1)
    t = pl.program_id(1)

    @pl.when(t == 0)
    def _():
        s_scr[...] = s0_ref[...]
        hist_scr[:, hist0:DN_HIST_OFF, :] = cbuf_ref[...]

    x = x_ref[...].reshape(m, d)
    h = _rms(x, gpre_ref[...]).astype(BF16)
    proj = _dot(h, wqkvg_ref[...])
    ab = _dot(h, wab_ref[...])
    hist_scr[:, DN_HIST_OFF:DN_HIST_OFF + tl, :] = proj[:, :3 * w].reshape(bb, tl, 3 * w)
    conv = cw_ref[0:1, :] * hist_scr[:, hist0:hist0 + tl, :]
    for j in range(1, DN_CONV_W):
        conv = conv + cw_ref[j:j + 1, :] * hist_scr[:, hist0 + j:hist0 + j + tl, :]
    qkv = _silu(conv).reshape(m, 3 * w)
    tail = hist_scr[:, tl + hist0:tl + DN_HIST_OFF, :]
    hist_scr[:, hist0:DN_HIST_OFF, :] = tail

    abt = ab.T
    g_t = -jnp.exp(alog_ref[...]) * jax.nn.softplus(abt + dtb_ref[...])
    pos = lax.broadcasted_iota(jnp.int32, g_t.shape, 1) & (grp - 1)
    gc_t = g_t
    step = 1
    while step < grp:
        gc_t = gc_t + jnp.where(pos >= step, pltpu.roll(gc_t, step, axis=1), 0.0)
        step *= 2
    gc_col = gc_t.T
    beta_col = jax.nn.sigmoid(ab)

    ri = lax.broadcasted_iota(jnp.int32, (c, c), 0)
    ci_ = lax.broadcasted_iota(jnp.int32, (c, c), 1)
    causal = ri >= ci_
    strict = ri > ci_
    last = grp - 1
    bf = lambda a: a.astype(BF16)

    def chunk_rows(a, ck):
        blk = a[ck * grp:(ck + 1) * grp]
        if grp == c:
            return blk
        return jnp.concatenate([blk, jnp.zeros((c - grp, a.shape[1]), a.dtype)], axis=0)

    qn, kn = [], []
    for hh in range(DN_HEADS):
        qh = qkv[:, hh * hd:(hh + 1) * hd]
        kh = qkv[:, w + hh * hd:w + (hh + 1) * hd]
        qn.append(qh * lax.rsqrt(jnp.sum(qh * qh, axis=-1, keepdims=True) + EPS) * (hd ** -0.5))
        kn.append(kh * lax.rsqrt(jnp.sum(kh * kh, axis=-1, keepdims=True) + EPS))

    probs = [(ck, hh) for ck in range(nck) for hh in range(DN_HEADS)]
    gc_c, eg_c, egl_c, ge_c, beta_c, gct_c = [], [], [], [], [], []
    for ck in range(nck):
        g = chunk_rows(gc_col, ck)
        g_last = g[last:last + 1, :]
        gc_c.append(g)
        eg_c.append(jnp.exp(g))
        egl_c.append(jnp.exp(g_last - g))
        ge_c.append(jnp.exp(g_last))
        beta_c.append(chunk_rows(beta_col, ck))
        row = gc_t[0:DN_HEADS, ck * grp:(ck + 1) * grp]
        gct_c.append(row if grp == c else jnp.concatenate([row, jnp.zeros((DN_HEADS, c - grp), F32)], axis=1))
    q_p = [chunk_rows(qn[hh], ck) for ck, hh in probs]
    k_p = [chunk_rows(kn[hh], ck) for ck, hh in probs]
    v_p = [chunk_rows(qkv[:, 2 * w + hh * hd:2 * w + (hh + 1) * hd], ck) for ck, hh in probs]
    beta_p = [beta_c[ck][:, DN_HEADS + hh:DN_HEADS + hh + 1] for ck, hh in probs]
    eg_p = [eg_c[ck][:, hh:hh + 1] for ck, hh in probs]
    dec_p = [jnp.where(causal, jnp.exp(jnp.minimum(gc_c[ck][:, hh:hh + 1] - gct_c[ck][hh:hh + 1, :], 0.0)), 0.0)
             for ck, hh in probs]
    kb_p = [k * beta for k, beta in zip(k_p, beta_p)]
    kkqk_p = [_dot_nt(bf(jnp.concatenate([kb, q], axis=0)), bf(k)) for kb, q, k in zip(kb_p, q_p, k_p)]
    a_p = [jnp.where(strict, kkqk[:c] * dec, 0.0) for kkqk, dec in zip(kkqk_p, dec_p)]
    qk_p = [bf(kkqk[c:] * dec) for kkqk, dec in zip(kkqk_p, dec_p)]
    tinv_p = _unit_lower_inverses(a_p, c)
    rhs_p = [bf(jnp.concatenate([v * beta, kb * eg], axis=1)) for v, beta, kb, eg in zip(v_p, beta_p, kb_p, eg_p)]
    sol_p = [_dot(bf(t_inv), rhs) for t_inv, rhs in zip(tinv_p, rhs_p)]
    kcq_p = [bf(jnp.concatenate([sol[:, hd:], q * eg], axis=0)) for sol, q, eg in zip(sol_p, q_p, eg_p)]
    kd_p = [bf(k * egl_c[ck][:, hh:hh + 1]) for k, (ck, hh) in zip(k_p, probs)]

    o_p = [None] * len(probs)
    for ci in range(nc):
        sel = [(b, hh, (b * nc + ci) * DN_HEADS + hh) for b in range(bb) for hh in range(DN_HEADS)]
        s_old = [s_scr[b, hh] for b, hh, _ in sel]
        r = [_dot(kcq_p[p], bf(s)) for (_, _, p), s in zip(sel, s_old)]
        ub = [bf(sol_p[p][:, :hd] - rr[:c]) for (_, _, p), rr in zip(sel, r)]
        for (b, hh, p), rr, u, s in zip(sel, r, ub, s_old):
            o_p[p] = rr[c:] + _dot(qk_p[p], u)
            s_scr[b, hh] = s * ge_c[p // DN_HEADS][:, hh:hh + 1] + _dot_tn(kd_p[p], u)

    o = jnp.concatenate(
        [jnp.concatenate(o_p[ck * DN_HEADS:(ck + 1) * DN_HEADS], axis=1)[0:grp] for ck in range(nck)], axis=0)
    gate = proj[:, 3 * w:]
    ng = ng_ref[...]
    outs = []
    for hh in range(DN_HEADS):
        sl = slice(hh * hd, (hh + 1) * hd)
        outs.append(_rms(o[:, sl], ng) * _silu(gate[:, sl]))
    y = _dot(jnp.concatenate(outs, axis=1).astype(BF16), wout_ref[...])
    o_ref[...] = (x + _rms(y, gpost_ref[...])).reshape(bb, tl, d)

    @pl.when(t == pl.num_programs(1) - 1)
    def _():
        sout_ref[...] = s_scr[...]
        cout_ref[...] = tail


def _dn(x, s0, conv_buf, norm_pre, norm_post, wqkvg, wab, cw, alog, dtb, ng, wout, i, j, bb, tl):
    b, l, d = x.shape
    hd, nh, w = DN_HEAD_DIM, DN_HEADS, DN_WIDTH
    nhist = DN_CONV_W - 1
    x_spec = pl.BlockSpec((bb, tl, d), lambda bi, ti: (bi, ti, 0))
    s_in_spec = pl.BlockSpec((None, bb, nh, hd, hd), lambda bi, ti: (j, bi, 0, 0, 0))
    c_in_spec = pl.BlockSpec((None, bb, nhist, 3 * w), lambda bi, ti: (j, bi, 0, 0))
    return pl.pallas_call(
        functools.partial(_dn_kernel, bb=bb, tl=tl),
        grid=(b // bb, l // tl),
        in_specs=[x_spec, s_in_spec, c_in_spec, _fixed((1, d), i, 1), _fixed((1, d), i, 1),
                  _fixed((d, 4 * w), j), _fixed((d, LANES), j), _fixed((DN_CONV_W, 3 * w), j),
                  _fixed((LANES, 1), j), _fixed((LANES, 1), j), _fixed((1, hd), j), _fixed((w, d), j)],
        out_specs=[x_spec,
                   pl.BlockSpec((bb, nh, hd, hd), lambda bi, ti: (bi, 0, 0, 0)),
                   pl.BlockSpec((bb, nhist, 3 * w), lambda bi, ti: (bi, 0, 0))],
        out_shape=[jax.ShapeDtypeStruct((b, l, d), F32),
                   jax.ShapeDtypeStruct((b, nh, hd, hd), F32),
                   jax.ShapeDtypeStruct((b, nhist, 3 * w), F32)],
        scratch_shapes=[pltpu.VMEM((bb, DN_HIST_OFF + tl, 3 * w), F32), pltpu.VMEM((bb, nh, hd, hd), F32)],
        compiler_params=_params("parallel", "arbitrary"),
        name=f"deltanet_l{i}",
    )(x, s0, conv_buf, norm_pre, norm_post, wqkvg, wab, cw, alog, dtb, ng, wout)


def _tiles(b, l):
    if l >= LONG_SEQ:
        return FFN_TILE_ROWS, 1, CA_TILE_ROWS, min(l, XA_TILE_ROWS), DN_TILE_SEQS, DN_TILE_ROWS
    bb = max(1, min(b, LANES // l))
    return min(b * l, FFN_TILE_ROWS), bb, l, l, bb, l


def _trunk(x, mem_k, mem_v, conv_bufs, dn_states, dn_bufs, P):
    b, l, d = x.shape
    depth = P["wq"].shape[0]
    tm, bb, tl, tl_xa, bb_dn, tl_dn = _tiles(b, l)
    ffn = lambda x, i, k, j: _ffn(x.reshape(b * l, d), P["norm_pre"], P["norm_post"], P["wg"], P["wu"], P["wd"],
                                  i, k, j, tm).reshape(b, l, d)
    new_conv, new_states, new_dn_bufs = [], [], []
    conv_args = (P["ca_win"], P["ca_b_in"], P["ca_dw"], P["ca_dw_b"], P["ca_ln_g"], P["ca_ln_b"], P["ca_wout"],
                 P["ca_b_out"])
    for i in range(depth):
        j = i // 2
        x = ffn(x, i, 0, 0)
        if i % 2 == 0:
            x, buf = _conv(x, conv_bufs, P["norm_pre"], P["norm_post"], *conv_args, i, j, bb, tl)
            new_conv.append(buf)
        else:
            x, s, buf = _dn(x, dn_states, dn_bufs, P["norm_pre"], P["norm_post"], P["dn_wqkvg"], P["dn_wab"],
                            P["dn_conv_w"], P["dn_alog"], P["dn_dtb"], P["dn_norm_g"], P["dn_wout"], i, j,
                            bb_dn, tl_dn)
            new_states.append(s)
            new_dn_bufs.append(buf)
        x = _xattn(x, mem_k, mem_v, P["norm_pre"], P["norm_post"], P["wq"], P["wo"], i, bb, tl_xa)
        x = ffn(x, i, 3, 1)
    return x, jnp.stack(new_conv), jnp.stack(new_states), jnp.stack(new_dn_bufs)


def _prep(norm_pre, norm_post, ffn_w_gate, ffn_w_up, ffn_w_down, xa_wq, xa_wo,
          ca_w_in, ca_b_in, ca_dw, ca_dw_b, ca_ln_g, ca_ln_b, ca_w_out, ca_b_out,
          dn_w_in, dn_conv_w, dn_a_log, dn_dt_bias, dn_norm_g, dn_w_out):
    w, nh = DN_WIDTH, DN_HEADS
    row = lambda p: p[..., None, :]
    col = lambda p: jnp.pad(p, ((0, 0), (0, LANES - nh)))[..., None]
    return dict(
        norm_pre=row(norm_pre), norm_post=row(norm_post),
        wg=ffn_w_gate.astype(BF16), wu=ffn_w_up.astype(BF16), wd=ffn_w_down.astype(BF16),
        wq=xa_wq.astype(BF16), wo=xa_wo.astype(BF16),
        ca_win=ca_w_in.astype(BF16), ca_b_in=row(ca_b_in), ca_dw=ca_dw, ca_dw_b=row(ca_dw_b),
        ca_ln_g=row(ca_ln_g), ca_ln_b=row(ca_ln_b), ca_wout=ca_w_out.astype(BF16), ca_b_out=row(ca_b_out),
        dn_wqkvg=dn_w_in[:, :, :4 * w].astype(BF16),
        dn_wab=jnp.pad(dn_w_in[:, :, 4 * w:], ((0, 0), (0, 0), (0, LANES - 2 * nh))).astype(BF16),
        dn_conv_w=dn_conv_w, dn_alog=col(dn_a_log), dn_dtb=col(dn_dt_bias), dn_norm_g=row(dn_norm_g),
        dn_wout=dn_w_out.astype(BF16),
    )


def kernel(x_prompt, x_sample, mem_prompt, cache_conv_a, state_dn, cache_dn_conv, cache_mem_k, cache_mem_v,
           norm_pre, norm_post, ffn_w_gate, ffn_w_up, ffn_w_down, xa_wq, xa_wk, xa_wv, xa_wo,
           ca_w_in, ca_b_in, ca_dw, ca_dw_b, ca_ln_g, ca_ln_b, ca_w_out, ca_b_out,
           dn_w_in, dn_conv_w, dn_a_log, dn_dt_bias, dn_norm_g, dn_w_out):
    bp = x_prompt.shape[0]
    n_conv, n_dn = ca_w_in.shape[0], dn_w_in.shape[0]
    P = _prep(norm_pre, norm_post, ffn_w_gate, ffn_w_up, ffn_w_down, xa_wq, xa_wo,
              ca_w_in, ca_b_in, ca_dw, ca_dw_b, ca_ln_g, ca_ln_b, ca_w_out, ca_b_out,
              dn_w_in, dn_conv_w, dn_a_log, dn_dt_bias, dn_norm_g, dn_w_out)
    p_k, p_v, p_k16, p_v16 = _memkv(mem_prompt, xa_wk.astype(BF16), xa_wv.astype(BF16), MEMKV_BATCH_ROWS)
    zero_conv = jnp.zeros((n_conv, bp) + cache_conv_a.shape[2:], F32)
    zero_state = jnp.zeros((n_dn, bp) + state_dn.shape[2:], F32)
    zero_dn_conv = jnp.zeros((n_dn, bp) + cache_dn_conv.shape[2:], F32)
    y_p, p_conv_a, p_state_dn, p_dn_conv = _trunk(x_prompt, p_k16, p_v16, zero_conv, zero_state, zero_dn_conv, P)
    y_s, s_conv_a, s_state_dn, s_dn_conv = _trunk(
        x_sample, cache_mem_k, cache_mem_v, cache_conv_a, state_dn, cache_dn_conv, P)
    return (y_p, y_s, p_conv_a, p_state_dn, p_dn_conv, p_k, p_v, s_conv_a, s_state_dn, s_dn_conv)
```
